```python
import jax, jax.numpy as jnp
from jax import lax
import numpy as np

D_MODEL = 1024
BATCH = 8
SEQ = 2048
DEPTH = 2
DEC_BATCH = 128
DEC_SEQ = 1
PAST_LEN = 2048
PAGE_SIZE = 128

SB_HEAD_DIM = 64
SB_WIDTH = D_MODEL // 2
SB_HEADS = SB_WIDTH // SB_HEAD_DIM
SB_BIAS_INIT = -7.0
POOL_WINDOWS = (2, 4, 8, 16)
POOL_WIDTH = D_MODEL // 4
POOL_GROUP = POOL_WIDTH // len(POOL_WINDOWS)
POOL_STATE = max(POOL_WINDOWS) - 1
CONV_WIDTH = D_MODEL // 4
CONV_K = 31
CONV_STATE = CONV_K - 1
MIX_WIDTH = SB_WIDTH + POOL_WIDTH + CONV_WIDTH
IN_WIDTH = 3 * SB_WIDTH + POOL_WIDTH + 2 * CONV_WIDTH
MEM_LEN = 256
MEM_HEADS = 4
MEM_HEAD_DIM = D_MODEL // MEM_HEADS
D_FF = 4 * D_MODEL
Q_BLOCK = 128
EPS = 1e-6

kernel_name = 'stick_pool_conv_hybrid_step'


def rms_norm(x, g):
    xf = x.astype(jnp.float32)
    y = xf * lax.rsqrt(jnp.mean(xf * xf, axis=-1, keepdims=True) + EPS)
    return (y * g.astype(jnp.float32)).astype(x.dtype)


def layer_norm(x, g, b):
    xf = x.astype(jnp.float32)
    mu = jnp.mean(xf, axis=-1, keepdims=True)
    xc = xf - mu
    var = jnp.mean(xc * xc, axis=-1, keepdims=True)
    return (xc * lax.rsqrt(var + EPS) * g.astype(jnp.float32) + b.astype(jnp.float32)).astype(x.dtype)


def stick_breaking_attention(q, k, v, q_pos, k_pos, bias):
    b, lq, h, dh = q.shape
    qb = Q_BLOCK if lq % Q_BLOCK == 0 else lq
    nblk = lq // qb
    scale = dh ** -0.5
    bias_f = bias.astype(jnp.float32)[None, :, None, None]

    def one_block(args):
        q_blk, p_blk = args
        z = jnp.einsum('bqhd,bkhd->bhqk', q_blk, k).astype(jnp.float32) * scale + bias_f
        visible = k_pos[None, :] < p_blk[:, None]
        log_stay = jnp.where(visible, jax.nn.log_sigmoid(-z), 0.0)
        later = lax.cumsum(log_stay, axis=3, reverse=True) - log_stay
        log_w = jnp.where(visible, jax.nn.log_sigmoid(z) + later, -jnp.inf)
        w = jnp.exp(log_w).astype(v.dtype)
        return jnp.einsum('bhqk,bkhd->bqhd', w, v)

    qs = q.reshape(b, nblk, qb, h, dh).swapaxes(0, 1)
    ps = q_pos.reshape(nblk, qb)
    out = lax.map(one_block, (qs, ps))
    return out.swapaxes(0, 1).reshape(b, lq, h, dh)


def pool_mix(u, prev, start_pos, w_grp, scale):
    L = u.shape[1]
    ext = jnp.concatenate([prev, u], axis=1)
    cs = jnp.cumsum(ext.astype(jnp.float32), axis=1)
    cs = jnp.concatenate([jnp.zeros_like(cs[:, :1]), cs], axis=1)
    pos = start_pos + jnp.arange(L)
    end = cs[:, POOL_STATE + 1:]
    outs = []
    for g, w in enumerate(POOL_WINDOWS):
        sl = slice(g * POOL_GROUP, (g + 1) * POOL_GROUP)
        begin = cs[:, POOL_STATE + 1 - w: POOL_STATE + 1 - w + L, sl]
        cnt = jnp.minimum(w, pos + 1).astype(jnp.float32)[None, :, None]
        diff = ((end[..., sl] - begin) / cnt - u[..., sl].astype(jnp.float32)).astype(u.dtype)
        outs.append(diff @ w_grp[g])
    y = jnp.concatenate(outs, axis=-1) * scale
    return y, ext[:, -POOL_STATE:]


def conformer_conv(a, gate, prev, w_dw, b_dw, ln_g, ln_b, w_pw):
    c = a * jax.nn.sigmoid(gate)
    ext = jnp.concatenate([prev, c], axis=1)
    y = lax.conv_general_dilated(ext, w_dw[:, None, :], window_strides=(1,), padding='VALID',
                                 dimension_numbers=('NWC', 'WIO', 'NWC'),
                                 feature_group_count=CONV_WIDTH) + b_dw
    y = jax.nn.silu(layer_norm(y, ln_g, ln_b))
    return y @ w_pw, ext[:, -CONV_STATE:]


def memory_attention(h, mk, mv, w_q, w_o):
    b, L, _ = h.shape
    q = (h @ w_q).reshape(b, L, MEM_HEADS, MEM_HEAD_DIM)
    s = jnp.einsum('blhd,bmhd->bhlm', q, mk).astype(jnp.float32) * (MEM_HEAD_DIM ** -0.5)
    p = jax.nn.softmax(s, axis=-1).astype(mv.dtype)
    o = jnp.einsum('bhlm,bmhd->blhd', p, mv).reshape(b, L, MEM_HEADS * MEM_HEAD_DIM)
    return o @ w_o


def decoder_layer(x, sb_prev_k, sb_prev_v, pool_prev, conv_prev, mem_k, mem_v, p):
    (g_mix, w_in, sb_bias, pool_w, pool_scale, conv_w, conv_b, conv_ln_g, conv_ln_b, conv_pw,
     w_out, g_mem, w_mq, w_mo, g_ffn, w_up, w_down) = p
    b, L, _ = x.shape
    start = sb_prev_k.shape[1]
    h = rms_norm(x, g_mix)
    zin = h @ w_in
    cuts = (SB_WIDTH, 2 * SB_WIDTH, 3 * SB_WIDTH, 3 * SB_WIDTH + POOL_WIDTH,
            3 * SB_WIDTH + POOL_WIDTH + CONV_WIDTH)
    q, k, v, u, ca, cg = jnp.split(zin, cuts, axis=-1)
    q = q.reshape(b, L, SB_HEADS, SB_HEAD_DIM)
    k = k.reshape(b, L, SB_HEADS, SB_HEAD_DIM)
    v = v.reshape(b, L, SB_HEADS, SB_HEAD_DIM)
    k_all = jnp.concatenate([sb_prev_k, k], axis=1)
    v_all = jnp.concatenate([sb_prev_v, v], axis=1)
    k_pos = jnp.arange(start + L)
    q_pos = start + jnp.arange(L)
    out_a = stick_breaking_attention(q, k_all, v_all, q_pos, k_pos, sb_bias).reshape(b, L, SB_WIDTH)
    out_b, pool_state = pool_mix(u, pool_prev, start, pool_w, pool_scale)
    out_c, conv_state = conformer_conv(ca, cg, conv_prev, conv_w, conv_b, conv_ln_g, conv_ln_b, conv_pw)
    x = x + jnp.concatenate([out_a, out_b, out_c], axis=-1) @ w_out
    x = x + memory_attention(rms_norm(x, g_mem), mem_k, mem_v, w_mq, w_mo)
    hf = rms_norm(x, g_ffn)
    x = x + jnp.square(jax.nn.relu(hf @ w_up)) @ w_down
    return x, k, v, pool_state, conv_state


def setup_inputs(seed: int = 0) -> dict:
    key = jax.random.key(seed)
    ks = iter(jax.random.split(key, 48))
    f32 = jnp.float32

    def nrm(shape, scale):
        return jax.random.normal(next(ks), shape, f32) * scale

    n_pages = PAST_LEN // PAGE_SIZE
    n_used = DEC_BATCH * n_pages
    n_phys = n_used + n_used // 4
    perm = jax.random.permutation(next(ks), n_phys)[:n_used]
    page_table = perm.reshape(DEC_BATCH, n_pages).astype(jnp.int32)

    return {
        'x_prompt': nrm((BATCH, SEQ, D_MODEL), 1.0),
        'x_sample': nrm((DEC_BATCH, DEC_SEQ, D_MODEL), 1.0),
        'cache_sb_k': nrm((DEPTH, n_phys, PAGE_SIZE, SB_HEADS, SB_HEAD_DIM), 1.0),
        'cache_sb_v': nrm((DEPTH, n_phys, PAGE_SIZE, SB_HEADS, SB_HEAD_DIM), 1.0),
        'cache_mem_k': nrm((DEPTH, DEC_BATCH, MEM_LEN, MEM_HEADS, MEM_HEAD_DIM), 1.0),
        'cache_mem_v': nrm((DEPTH, DEC_BATCH, MEM_LEN, MEM_HEADS, MEM_HEAD_DIM), 1.0),
        'state_pool': nrm((DEPTH, DEC_BATCH, POOL_STATE, POOL_WIDTH), 1.0),
        'state_conv': nrm((DEPTH, DEC_BATCH, CONV_STATE, CONV_WIDTH), 1.0),
        'page_table': page_table,
        'mem_prompt': nrm((BATCH, MEM_LEN, D_MODEL), 1.0),
        'norm_mix_g': 1.0 + nrm((DEPTH, D_MODEL), 0.02),
        'w_in': nrm((DEPTH, D_MODEL, IN_WIDTH), D_MODEL ** -0.5),
        'sb_bias': SB_BIAS_INIT + nrm((DEPTH, SB_HEADS), 0.1),
        'pool_w': nrm((DEPTH, len(POOL_WINDOWS), POOL_GROUP, POOL_GROUP), POOL_GROUP ** -0.5),
        'pool_scale': 1.0 + nrm((DEPTH, POOL_WIDTH), 0.02),
        'conv_w': nrm((DEPTH, CONV_K, CONV_WIDTH), CONV_K ** -0.5),
        'conv_b': nrm((DEPTH, CONV_WIDTH), 0.02),
        'conv_ln_g': 1.0 + nrm((DEPTH, CONV_WIDTH), 0.02),
        'conv_ln_b': nrm((DEPTH, CONV_WIDTH), 0.02),
        'conv_pw': nrm((DEPTH, CONV_WIDTH, CONV_WIDTH), CONV_WIDTH ** -0.5),
        'w_out': nrm((DEPTH, MIX_WIDTH, D_MODEL), MIX_WIDTH ** -0.5),
        'norm_mem_g': 1.0 + nrm((DEPTH, D_MODEL), 0.02),
        'w_mq': nrm((DEPTH, D_MODEL, MEM_HEADS * MEM_HEAD_DIM), D_MODEL ** -0.5),
        'w_mk': nrm((DEPTH, D_MODEL, MEM_HEADS * MEM_HEAD_DIM), D_MODEL ** -0.5),
        'w_mv': nrm((DEPTH, D_MODEL, MEM_HEADS * MEM_HEAD_DIM), D_MODEL ** -0.5),
        'w_mo': nrm((DEPTH, MEM_HEADS * MEM_HEAD_DIM, D_MODEL), D_MODEL ** -0.5),
        'norm_ffn_g': 1.0 + nrm((DEPTH, D_MODEL), 0.02),
        'w_up': nrm((DEPTH, D_MODEL, D_FF), D_MODEL ** -0.5),
        'w_down': nrm((DEPTH, D_FF, D_MODEL), D_FF ** -0.5),
        'norm_final_g': 1.0 + nrm((D_MODEL,), 0.02),
    }


def reference(x_prompt, x_sample, cache_sb_k, cache_sb_v, cache_mem_k, cache_mem_v,
              state_pool, state_conv, page_table, mem_prompt,
              norm_mix_g, w_in, sb_bias, pool_w, pool_scale, conv_w, conv_b, conv_ln_g, conv_ln_b,
              conv_pw, w_out, norm_mem_g, w_mq, w_mk, w_mv, w_mo, norm_ffn_g, w_up, w_down,
              norm_final_g):
    bp = x_prompt.shape[0]
    bs = x_sample.shape[0]
    n_pages = page_table.shape[1]
    past_len = n_pages * PAGE_SIZE
    dt_p = x_prompt.dtype
    xp, xs = x_prompt, x_sample
    sbk_p, sbv_p, sbk_s, sbv_s = [], [], [], []
    mk_out, mv_out = [], []
    pool_p, pool_s, conv_p, conv_s = [], [], [], []
    for l in range(DEPTH):
        p = (norm_mix_g[l], w_in[l], sb_bias[l], pool_w[l], pool_scale[l], conv_w[l], conv_b[l],
             conv_ln_g[l], conv_ln_b[l], conv_pw[l], w_out[l], norm_mem_g[l], w_mq[l],
             w_mo[l], norm_ffn_g[l], w_up[l], w_down[l])
        mk_p = (mem_prompt @ w_mk[l]).reshape(bp, -1, MEM_HEADS, MEM_HEAD_DIM)
        mv_p = (mem_prompt @ w_mv[l]).reshape(bp, -1, MEM_HEADS, MEM_HEAD_DIM)
        empty = jnp.zeros((bp, 0, SB_HEADS, SB_HEAD_DIM), dt_p)
        xp, kp, vp, pst, cst = decoder_layer(
            xp, empty, empty,
            jnp.zeros((bp, POOL_STATE, POOL_WIDTH), dt_p),
            jnp.zeros((bp, CONV_STATE, CONV_WIDTH), dt_p),
            mk_p, mv_p, p)
        sbk_p.append(kp); sbv_p.append(vp); mk_out.append(mk_p); mv_out.append(mv_p)
        pool_p.append(pst); conv_p.append(cst)
        past_k = cache_sb_k[l][page_table].reshape(bs, past_len, SB_HEADS, SB_HEAD_DIM)
        past_v = cache_sb_v[l][page_table].reshape(bs, past_len, SB_HEADS, SB_HEAD_DIM)
        xs, ks_, vs_, pss, css = decoder_layer(
            xs, past_k, past_v, state_pool[l], state_conv[l],
            cache_mem_k[l], cache_mem_v[l], p)
        sbk_s.append(ks_); sbv_s.append(vs_); pool_s.append(pss); conv_s.append(css)
    y_prompt = rms_norm(xp, norm_final_g)
    y_sample = rms_norm(xs, norm_final_g)
    return (y_prompt, y_sample, jnp.stack(sbk_p), jnp.stack(sbv_p), jnp.stack(sbk_s),
            jnp.stack(sbv_s), jnp.stack(mk_out), jnp.stack(mv_out), jnp.stack(pool_p),
            jnp.stack(pool_s), jnp.stack(conv_p), jnp.stack(conv_s))
```

```python
import functools

import jax
import jax.numpy as jnp
from jax import lax
from jax.experimental import pallas as pl
from jax.experimental.pallas import tpu as pltpu

F32 = jnp.float32
BF16 = jnp.bfloat16

D_MODEL = 1024
DEPTH = 2
PAGE_SIZE = 128
SB_HEAD_DIM = 64
SB_WIDTH = D_MODEL // 2
SB_HEADS = SB_WIDTH // SB_HEAD_DIM
POOL_WINDOWS = (2, 4, 8, 16)
POOL_WIDTH = D_MODEL // 4
POOL_GROUP = POOL_WIDTH // len(POOL_WINDOWS)
POOL_STATE = max(POOL_WINDOWS) - 1
CONV_WIDTH = D_MODEL // 4
CONV_K = 31
CONV_STATE = CONV_K - 1
MEM_HEADS = 4
MEM_HEAD_DIM = D_MODEL // MEM_HEADS
D_FF = 4 * D_MODEL
EPS = 1e-6

SB_SCALE = SB_HEAD_DIM ** -0.5
MEM_SCALE = MEM_HEAD_DIM ** -0.5

LANES = 128
SB_TILE = 256
ROW_TILE = 512
SEQ_TILE = 512
HALO = 32
CONV_CHUNK = 64
FF_CHUNK = 1024


def _const_spec(shape):
    nd = len(shape)
    return pl.BlockSpec(shape, lambda *_: (0,) * nd, pipeline_mode=pl.Buffered(1))


def _rms(x, g):
    return x * lax.rsqrt(jnp.mean(x * x, axis=-1, keepdims=True) + EPS) * g


def _softplus(z):
    return jnp.maximum(z, 0.0) + jnp.log(1.0 + jnp.exp(-jnp.abs(z)))


def _dot(a, b):
    return jnp.dot(a, b, preferred_element_type=F32)


def _dot_nt(a, b):
    return lax.dot_general(a, b, (((1,), (1,)), ((), ())), preferred_element_type=F32)


def _mem_kv_kernel(m_ref, wk_ref, wv_ref, k_ref, v_ref, kb_ref, vb_ref):
    m = m_ref[...].astype(BF16)
    k = _dot(m, wk_ref[...])
    v = _dot(m, wv_ref[...])
    k_ref[...] = k
    v_ref[...] = v
    kb_ref[...] = k.astype(BF16)
    vb_ref[...] = v.astype(BF16)


def _mem_kv(mem_rows, wk, wv):
    rows = mem_rows.shape[0]
    tm = min(ROW_TILE, rows)
    w_spec = pl.BlockSpec((None, D_MODEL, D_MODEL), lambda l, i: (l, 0, 0))
    o_spec = pl.BlockSpec((None, tm, D_MODEL), lambda l, i: (l, i, 0))
    f = jax.ShapeDtypeStruct((DEPTH, rows, D_MODEL), F32)
    b = jax.ShapeDtypeStruct((DEPTH, rows, D_MODEL), BF16)
    return pl.pallas_call(
        _mem_kv_kernel,
        grid=(DEPTH, rows // tm),
        in_specs=[pl.BlockSpec((tm, D_MODEL), lambda l, i: (i, 0)), w_spec, w_spec],
        out_specs=[o_spec, o_spec, o_spec, o_spec],
        out_shape=[f, f, b, b],
        name="mem_kv",
    )(mem_rows, wk, wv)


def _in_proj_kernel(x_ref, g_ref, wq_ref, wkt_ref, wvt_ref, wr_ref, q_ref, kt_ref, vt_ref, ktb_ref, vtb_ref,
                    u_ref, c_ref):
    h = _rms(x_ref[...], g_ref[...]).astype(BF16)
    q_ref[...] = (_dot(h, wq_ref[...]) * SB_SCALE).astype(BF16)
    kt = _dot_nt(wkt_ref[...], h)
    kt_ref[...] = kt
    ktb_ref[...] = kt.astype(BF16)
    vt = _dot_nt(wvt_ref[...], h)
    vt_ref[...] = vt
    vtb_ref[...] = vt.astype(BF16)
    u_ref[...] = _dot(h, wr_ref[:, 0:POOL_WIDTH])
    glu_a = _dot(h, wr_ref[:, POOL_WIDTH:POOL_WIDTH + CONV_WIDTH])
    glu_g = _dot(h, wr_ref[:, POOL_WIDTH + CONV_WIDTH:])
    c_ref[...] = glu_a * jax.nn.sigmoid(glu_g)


def _in_proj(x, g, wq, wkt, wvt, wr):
    b, seq, _ = x.shape
    tm = min(ROW_TILE, seq)

    def row_spec(width):
        return pl.BlockSpec((None, tm, width), lambda bi, i: (bi, i, 0))

    t_spec = pl.BlockSpec((None, SB_WIDTH, tm), lambda bi, i: (bi, 0, i))

    def rows(width, dtype):
        return jax.ShapeDtypeStruct((b, seq, width), dtype)

    def cols(dtype):
        return jax.ShapeDtypeStruct((b, SB_WIDTH, seq), dtype)

    return pl.pallas_call(
        _in_proj_kernel,
        grid=(b, seq // tm),
        in_specs=[row_spec(D_MODEL), _const_spec((1, D_MODEL)), _const_spec(wq.shape), _const_spec(wkt.shape),
                  _const_spec(wvt.shape), _const_spec(wr.shape)],
        out_specs=[row_spec(SB_WIDTH), t_spec, t_spec, t_spec, t_spec, row_spec(POOL_WIDTH),
                   row_spec(CONV_WIDTH)],
        out_shape=[rows(SB_WIDTH, BF16), cols(F32), cols(F32), cols(BF16), cols(BF16), rows(POOL_WIDTH, F32),
                   rows(CONV_WIDTH, F32)],
        name="in_proj",
    )(x, g, wq, wkt, wvt, wr)


def _sb_prompt_kernel(bias_ref, q_ref, k_ref, v_ref, tri_ref, o_ref, acc_ref):
    t = SB_TILE
    hp = pl.program_id(1)
    i = pl.program_id(2)
    q = q_ref[0].astype(F32)
    first_head = lax.broadcasted_iota(jnp.int32, (t, LANES), 1) < SB_HEAD_DIM
    q_heads = (jnp.where(first_head, q, 0.0).astype(BF16), jnp.where(first_head, 0.0, q).astype(BF16))
    biases = (bias_ref[2 * hp], bias_ref[2 * hp + 1])
    visible = (lax.broadcasted_iota(jnp.int32, (t, t), 1) < lax.broadcasted_iota(jnp.int32, (t, t), 0))
    acc_ref[...] = jnp.zeros_like(acc_ref)

    def key_tile(j, carries, masked):
        start = pl.multiple_of(j * t, t)
        ks = k_ref[0, :, pl.ds(start, t)]
        vs = v_ref[0, :, pl.ds(start, t)]
        new = []
        for h in range(2):
            z = _dot(q_heads[h], ks) + biases[h]
            sp = _softplus(z)
            if masked:
                sp = jnp.where(visible, sp, 0.0)
            later = _dot(sp.astype(BF16), tri_ref[...])
            w = jnp.exp((z - sp) + later + carries[h])
            if masked:
                w = jnp.where(visible, w, 0.0)
            acc_ref[h] += _dot_nt(w.astype(BF16), vs)
            new.append(carries[h] + later[:, :1] - sp[:, :1])
        return tuple(new)

    zero_carry = jnp.zeros((t, 1), F32)
    carries = key_tile(i, (zero_carry, zero_carry), True)
    lax.fori_loop(0, i, lambda jj, c: key_tile(i - 1 - jj, c, False), carries)
    o_ref[0] = jnp.where(first_head, acc_ref[0], acc_ref[1]).astype(BF16)


def _sb_prompt(bias, q, kt, vt, tri):
    b, seq, _ = q.shape
    t = SB_TILE
    kv_spec = pl.BlockSpec((1, LANES, seq), lambda bi, hp, i: (bi, hp, 0))
    qo_spec = pl.BlockSpec((1, t, LANES), lambda bi, hp, i: (bi, i, hp))
    return pl.pallas_call(
        _sb_prompt_kernel,
        grid=(b, SB_WIDTH // LANES, seq // t),
        in_specs=[pl.BlockSpec(memory_space=pltpu.SMEM), qo_spec, kv_spec, kv_spec, _const_spec((t, t))],
        out_specs=qo_spec,
        out_shape=jax.ShapeDtypeStruct((b, seq, SB_WIDTH), BF16),
        scratch_shapes=[pltpu.VMEM((2, t, LANES), F32)],
        name="sb_prompt",
    )(bias, q, kt, vt, tri)


def _sb_sample_kernel(n_pages, pt_ref, q_ref, bias_ref, tri_ref, pages_ref, *refs):
    del pt_ref
    k_refs = refs[:n_pages]
    v_refs = refs[n_pages:2 * n_pages]
    o_ref = refs[2 * n_pages]
    head_of_lane = lax.broadcasted_iota(jnp.int32, (SB_HEADS, SB_WIDTH), 1) // SB_HEAD_DIM
    own = head_of_lane == lax.broadcasted_iota(jnp.int32, (SB_HEADS, SB_WIDTH), 0)
    q = jnp.broadcast_to(q_ref[0].astype(F32), (SB_HEADS, SB_WIDTH))
    q_heads = jnp.where(own, q, 0.0).astype(BF16)
    z = jnp.concatenate([_dot(q_heads, k_refs[p][...].astype(BF16)) for p in range(n_pages)], axis=0)
    z = z + bias_ref[...]
    sp = _softplus(z)
    sp_b = sp.astype(BF16)
    later_in_page = _dot(sp_b, tri_ref[...])
    later_pages = jnp.sum(_dot(pages_ref[...], sp_b), axis=1, keepdims=True)
    w = jnp.exp((z - sp) + later_in_page + later_pages)
    acc = jnp.zeros((SB_HEADS, SB_WIDTH), F32)
    for p in range(n_pages):
        acc = acc + _dot_nt(w[p * SB_HEADS:(p + 1) * SB_HEADS].astype(BF16), v_refs[p][...].astype(BF16))
    o_ref[0] = jnp.sum(jnp.where(own, acc, 0.0), axis=0, keepdims=True).astype(BF16)


def _sb_sample(layer, page_table, q, bias_rows, tri, pages_mat, cache_k, cache_v):
    bs, n_pages = page_table.shape

    def page_spec(p):
        return pl.BlockSpec((None, None, SB_WIDTH, PAGE_SIZE), lambda b, pt: (layer, pt[b, p], 0, 0))

    rows = n_pages * SB_HEADS
    vec_spec = pl.BlockSpec((1, 1, SB_WIDTH), lambda b, pt: (b, 0, 0))
    grid_spec = pltpu.PrefetchScalarGridSpec(
        num_scalar_prefetch=1,
        grid=(bs,),
        in_specs=[vec_spec, _const_spec((rows, LANES)), _const_spec((PAGE_SIZE, PAGE_SIZE)),
                  _const_spec((rows, rows))]
        + [page_spec(p) for p in range(n_pages)] + [page_spec(p) for p in range(n_pages)],
        out_specs=vec_spec,
    )
    return pl.pallas_call(
        functools.partial(_sb_sample_kernel, n_pages),
        grid_spec=grid_spec,
        out_shape=jax.ShapeDtypeStruct((bs, 1, SB_WIDTH), BF16),
        name="sb_sample",
    )(page_table, q, bias_rows, tri, pages_mat, *([cache_k] * n_pages), *([cache_v] * n_pages))


def _layer_norm_silu(y, g, b):
    mu = jnp.mean(y, axis=-1, keepdims=True)
    yc = y - mu
    var = jnp.mean(yc * yc, axis=-1, keepdims=True)
    y = yc * lax.rsqrt(var + EPS) * g + b
    return y * jax.nn.sigmoid(y)


def _pool_conv_prompt_kernel(u_ref, c_ref, wp_ref, ps_ref, wdw_ref, bdw_ref, lng_ref, lnb_ref, pw_ref,
                             o_ref, us, cs, ys):
    tl = SEQ_TILE
    ti = pl.program_id(1)

    @pl.when(ti == 0)
    def _():
        us[0:HALO] = jnp.zeros((HALO, POOL_WIDTH), F32)
        cs[0:HALO] = jnp.zeros((HALO, CONV_WIDTH), F32)

    @pl.when(ti > 0)
    def _():
        us[0:HALO] = us[tl:tl + HALO]
        cs[0:HALO] = cs[tl:tl + HALO]

    u = u_ref[0]
    us[HALO:HALO + tl] = u
    cs[HALO:HALO + tl] = c_ref[0]

    def back(k, lo):
        return us[pl.ds(HALO - k, tl), lo:lo + LANES]

    s2 = back(0, 0) + back(1, 0)
    s4 = s2 + back(2, 0) + back(3, 0)
    s8 = back(0, LANES)
    for k in range(1, 8):
        s8 = s8 + back(k, LANES)
    s16 = s8
    for k in range(8, 16):
        s16 = s16 + back(k, LANES)
    low_group = lax.broadcasted_iota(jnp.int32, (tl, LANES), 1) < POOL_GROUP
    seen = ti * tl + lax.broadcasted_iota(jnp.int32, (tl, LANES), 0) + 1
    cnt_lo = jnp.minimum(jnp.where(low_group, POOL_WINDOWS[0], POOL_WINDOWS[1]), seen).astype(F32)
    cnt_hi = jnp.minimum(jnp.where(low_group, POOL_WINDOWS[2], POOL_WINDOWS[3]), seen).astype(F32)
    mean = jnp.concatenate([jnp.where(low_group, s2, s4) / cnt_lo,
                            jnp.where(low_group, s8, s16) / cnt_hi], axis=1)
    pool = _dot((mean - u).astype(BF16), wp_ref[...]) * ps_ref[...]

    for r0 in range(0, tl, CONV_CHUNK):
        acc = jnp.broadcast_to(bdw_ref[...], (CONV_CHUNK, CONV_WIDTH))
        for k in range(CONV_K):
            acc = acc + cs[pl.ds(r0 + HALO - CONV_STATE + k, CONV_CHUNK), :] * wdw_ref[k:k + 1, :]
        ys[r0:r0 + CONV_CHUNK] = acc
    conv = _dot(_layer_norm_silu(ys[...], lng_ref[...], lnb_ref[...]).astype(BF16), pw_ref[...])
    o_ref[0] = jnp.concatenate([pool, conv], axis=1).astype(BF16)


def _pool_conv_prompt(u, c, wp, ps, wdw, bdw, lng, lnb, pw):
    b, seq, _ = u.shape
    tl = SEQ_TILE
    vec = _const_spec((1, CONV_WIDTH))
    return pl.pallas_call(
        _pool_conv_prompt_kernel,
        grid=(b, seq // tl),
        in_specs=[pl.BlockSpec((1, tl, POOL_WIDTH), lambda bi, i: (bi, i, 0)),
                  pl.BlockSpec((1, tl, CONV_WIDTH), lambda bi, i: (bi, i, 0)),
                  _const_spec(wp.shape), vec, _const_spec(wdw.shape), vec, vec, vec, _const_spec(pw.shape)],
        out_specs=pl.BlockSpec((1, tl, POOL_WIDTH + CONV_WIDTH), lambda bi, i: (bi, i, 0)),
        out_shape=jax.ShapeDtypeStruct((b, seq, POOL_WIDTH + CONV_WIDTH), BF16),
        scratch_shapes=[pltpu.VMEM((HALO + tl, POOL_WIDTH), F32), pltpu.VMEM((HALO + tl, CONV_WIDTH), F32),
                        pltpu.VMEM((tl, CONV_WIDTH), F32)],
        compiler_params=pltpu.CompilerParams(dimension_semantics=("arbitrary", "arbitrary")),
        name="pool_conv_prompt",
    )(u, c, wp, ps, wdw, bdw, lng, lnb, pw)


def _pool_conv_sample_kernel(u_ref, c_ref, pprev_ref, cprev_ref, pwin_ref, wp_ref, ps_ref, wdw_ref, bdw_ref,
                             lng_ref, lnb_ref, pw_ref, o_ref):
    u = u_ref[...]
    c = c_ref[...]
    mean = u * pwin_ref[POOL_STATE:POOL_STATE + 1, :]
    for k in range(POOL_STATE):
        mean = mean + pprev_ref[k] * pwin_ref[k:k + 1, :]
    pool = _dot((mean - u).astype(BF16), wp_ref[...]) * ps_ref[...]
    y = c * wdw_ref[CONV_STATE:CONV_K, :] + bdw_ref[...]
    for k in range(CONV_STATE):
        y = y + cprev_ref[k] * wdw_ref[k:k + 1, :]
    conv = _dot(_layer_norm_silu(y, lng_ref[...], lnb_ref[...]).astype(BF16), pw_ref[...])
    o_ref[...] = jnp.concatenate([pool, conv], axis=1).astype(BF16)


def _pool_conv_sample(layer, u, c, pool_rows, conv_rows, pwin, wp, ps, wdw, bdw, lng, lnb, pw):
    bs = u.shape[0]
    tb = 32
    vec = _const_spec((1, CONV_WIDTH))
    return pl.pallas_call(
        _pool_conv_sample_kernel,
        grid=(bs // tb,),
        in_specs=[pl.BlockSpec((tb, POOL_WIDTH), lambda i: (i, 0)),
                  pl.BlockSpec((tb, CONV_WIDTH), lambda i: (i, 0)),
                  pl.BlockSpec((None, POOL_STATE, tb, POOL_WIDTH), lambda i: (layer, 0, i, 0)),
                  pl.BlockSpec((None, CONV_STATE, tb, CONV_WIDTH), lambda i: (layer, 0, i, 0)),
                  _const_spec(pwin.shape), _const_spec(wp.shape), vec, _const_spec(wdw.shape), vec, vec, vec,
                  _const_spec(pw.shape)],
        out_specs=pl.BlockSpec((tb, POOL_WIDTH + CONV_WIDTH), lambda i: (i, 0)),
        out_shape=jax.ShapeDtypeStruct((bs, POOL_WIDTH + CONV_WIDTH), BF16),
        name="pool_conv_sample",
    )(u, c, pool_rows, conv_rows, pwin, wp, ps, wdw, bdw, lng, lnb, pw)


def _out_proj_kernel(x_ref, a_ref, bc_ref, wo_ref, g_ref, wq_ref, x1_ref, q_ref):
    x1 = x_ref[...] + _dot(a_ref[...], wo_ref[0:SB_WIDTH, :]) + _dot(bc_ref[...], wo_ref[SB_WIDTH:, :])
    x1_ref[...] = x1
    h = _rms(x1, g_ref[...]).astype(BF16)
    q_ref[...] = (_dot(h, wq_ref[...]) * MEM_SCALE).astype(BF16)


def _out_proj(x, a, bc, wo, g, wq):
    rows = x.shape[0]
    tm = min(ROW_TILE, rows)

    def row_spec(width):
        return pl.BlockSpec((tm, width), lambda i: (i, 0))

    return pl.pallas_call(
        _out_proj_kernel,
        grid=(rows // tm,),
        in_specs=[row_spec(D_MODEL), row_spec(SB_WIDTH), row_spec(POOL_WIDTH + CONV_WIDTH),
                  _const_spec(wo.shape), _const_spec((1, D_MODEL)), _const_spec(wq.shape)],
        out_specs=[row_spec(D_MODEL), row_spec(D_MODEL)],
        out_shape=[jax.ShapeDtypeStruct((rows, D_MODEL), F32), jax.ShapeDtypeStruct((rows, D_MODEL), BF16)],
        name="out_proj",
    )(x, a, bc, wo, g, wq)


def _mem_attn_prompt_kernel(q_ref, mk_ref, mv_ref, o_ref):
    outs = []
    for h in range(MEM_HEADS):
        hs = slice(h * MEM_HEAD_DIM, (h + 1) * MEM_HEAD_DIM)
        s = _dot_nt(q_ref[0, :, hs], mk_ref[:, hs])
        e = jnp.exp(s - jnp.max(s, axis=-1, keepdims=True))
        den = jnp.sum(e, axis=-1, keepdims=True)
        outs.append(_dot((e / den).astype(BF16), mv_ref[:, hs]))
    o_ref[0] = jnp.concatenate(outs, axis=1).astype(BF16)


def _mem_attn_prompt(layer, q, mk, mv):
    b, seq, _ = q.shape
    tm = ROW_TILE
    mem_len = mk.shape[2]
    m_spec = pl.BlockSpec((None, None, mem_len, D_MODEL), lambda bi, i: (layer, bi, 0, 0))
    qo_spec = pl.BlockSpec((1, tm, D_MODEL), lambda bi, i: (bi, i, 0))
    return pl.pallas_call(
        _mem_attn_prompt_kernel,
        grid=(b, seq // tm),
        in_specs=[qo_spec, m_spec, m_spec],
        out_specs=qo_spec,
        out_shape=jax.ShapeDtypeStruct((b, seq, D_MODEL), BF16),
        name="mem_attn_prompt",
    )(q, mk, mv)


def _mem_attn_sample_kernel(q_ref, mk_ref, mv_ref, o_ref):
    head_of_lane = lax.broadcasted_iota(jnp.int32, (MEM_HEADS, D_MODEL), 1) // MEM_HEAD_DIM
    own = head_of_lane == lax.broadcasted_iota(jnp.int32, (MEM_HEADS, D_MODEL), 0)
    q = jnp.broadcast_to(q_ref[0].astype(F32), (MEM_HEADS, D_MODEL))
    q_heads = jnp.where(own, q, 0.0).astype(BF16)
    s = _dot_nt(q_heads, mk_ref[...].astype(BF16))
    e = jnp.exp(s - jnp.max(s, axis=-1, keepdims=True))
    den = jnp.sum(e, axis=-1, keepdims=True)
    o = _dot((e / den).astype(BF16), mv_ref[...].astype(BF16))
    o_ref[0] = jnp.sum(jnp.where(own, o, 0.0), axis=0, keepdims=True).astype(BF16)


def _mem_attn_sample(layer, q, cache_k, cache_v):
    bs = q.shape[0]
    mem_len = cache_k.shape[2]
    m_spec = pl.BlockSpec((None, None, mem_len, D_MODEL), lambda b: (layer, b, 0, 0))
    vec_spec = pl.BlockSpec((1, 1, D_MODEL), lambda b: (b, 0, 0))
    return pl.pallas_call(
        _mem_attn_sample_kernel,
        grid=(bs,),
        in_specs=[vec_spec, m_spec, m_spec],
        out_specs=vec_spec,
        out_shape=jax.ShapeDtypeStruct((bs, 1, D_MODEL), BF16),
        name="mem_attn_sample",
    )(q, cache_k, cache_v)


def _mo_ffn_kernel(final, x1_ref, o_ref, wmo_ref, g_ref, wup_ref, wdn_ref, *refs):
    out_ref = refs[-1]
    x2 = x1_ref[...] + _dot(o_ref[...], wmo_ref[...])
    h = _rms(x2, g_ref[...]).astype(BF16)
    acc = x2
    for c0 in range(0, D_FF, FF_CHUNK):
        up = jnp.maximum(_dot(h, wup_ref[:, c0:c0 + FF_CHUNK]), 0.0)
        acc = acc + _dot((up * up).astype(BF16), wdn_ref[c0:c0 + FF_CHUNK, :])
    if final:
        acc = _rms(acc, refs[0][...])
    out_ref[...] = acc


def _mo_ffn(x1, o, wmo, g, wup, wdn, g_final=None):
    rows = x1.shape[0]
    tm = min(ROW_TILE, rows)
    row_spec = pl.BlockSpec((tm, D_MODEL), lambda i: (i, 0))
    final = g_final is not None
    in_specs = [row_spec, row_spec, _const_spec(wmo.shape), _const_spec((1, D_MODEL)), _const_spec(wup.shape),
                _const_spec(wdn.shape)]
    args = [x1, o, wmo, g, wup, wdn]
    if final:
        in_specs.append(_const_spec((1, D_MODEL)))
        args.append(g_final)
    return pl.pallas_call(
        functools.partial(_mo_ffn_kernel, final),
        grid=(rows // tm,),
        in_specs=in_specs,
        out_specs=row_spec,
        out_shape=jax.ShapeDtypeStruct((rows, D_MODEL), F32),
        name="mo_ffn",
    )(*args)


def _suffix_triangle(n):
    j = lax.broadcasted_iota(jnp.int32, (n, n), 0)
    s = lax.broadcasted_iota(jnp.int32, (n, n), 1)
    return jnp.where(j > s, -1.0, 0.0).astype(BF16)


def _later_pages_matrix(n_pages):
    n = n_pages * SB_HEADS
    r = lax.broadcasted_iota(jnp.int32, (n, n), 0)
    c = lax.broadcasted_iota(jnp.int32, (n, n), 1)
    same_head = (r % SB_HEADS) == (c % SB_HEADS)
    return jnp.where(same_head & (c // SB_HEADS > r // SB_HEADS), -1.0, 0.0).astype(BF16)


def _pool_window_rows():
    row = lax.broadcasted_iota(jnp.int32, (POOL_STATE + 1, POOL_WIDTH), 0)
    ch = lax.broadcasted_iota(jnp.int32, (POOL_STATE + 1, POOL_WIDTH), 1)
    w = jnp.zeros((POOL_STATE + 1, POOL_WIDTH), jnp.int32)
    for g, win in enumerate(POOL_WINDOWS):
        w = jnp.where(ch // POOL_GROUP == g, win, w)
    return jnp.where(row >= POOL_STATE + 1 - w, 1.0 / w.astype(F32), 0.0)


def _block_diag(w_grp):
    n, g, _ = w_grp.shape
    out = jnp.zeros((n * g, n * g), w_grp.dtype)
    for i in range(n):
        out = lax.dynamic_update_slice(out, w_grp[i], (i * g, i * g))
    return out


def kernel(x_prompt, x_sample, cache_sb_k, cache_sb_v, cache_mem_k, cache_mem_v, state_pool, state_conv,
           page_table, mem_prompt, norm_mix_g, w_in, sb_bias, pool_w, pool_scale, conv_w, conv_b, conv_ln_g,
           conv_ln_b, conv_pw, w_out, norm_mem_g, w_mq, w_mk, w_mv, w_mo, norm_ffn_g, w_up, w_down,
           norm_final_g):
    bp, seq, _ = x_prompt.shape
    bs = x_sample.shape[0]
    n_pages = page_table.shape[1]
    n_phys = cache_sb_k.shape[1]
    mem_len = mem_prompt.shape[1]
    assert x_sample.shape[1] == 1 and seq % SB_TILE == 0 and seq >= CONV_STATE

    cache_k = cache_sb_k.transpose(0, 1, 3, 4, 2).reshape(DEPTH, n_phys, SB_WIDTH, PAGE_SIZE)
    cache_v = cache_sb_v.transpose(0, 1, 3, 4, 2).reshape(DEPTH, n_phys, SB_WIDTH, PAGE_SIZE)
    pool_rows = state_pool.transpose(0, 2, 1, 3)
    conv_rows = state_conv.transpose(0, 2, 1, 3)
    cmem_k = cache_mem_k.reshape(DEPTH, bs, mem_len, D_MODEL)
    cmem_v = cache_mem_v.reshape(DEPTH, bs, mem_len, D_MODEL)

    tri_prompt = _suffix_triangle(SB_TILE)
    tri_page = _suffix_triangle(PAGE_SIZE)
    pages_mat = _later_pages_matrix(n_pages)
    pwin = _pool_window_rows()

    mk_all, mv_all, mkb_all, mvb_all = _mem_kv(mem_prompt.reshape(bp * mem_len, D_MODEL),
                                                w_mk.astype(BF16), w_mv.astype(BF16))
    mkb_all = mkb_all.reshape(DEPTH, bp, mem_len, D_MODEL)
    mvb_all = mvb_all.reshape(DEPTH, bp, mem_len, D_MODEL)

    def heads_last(t):
        return t.reshape(t.shape[0], SB_HEADS, SB_HEAD_DIM, t.shape[2]).transpose(0, 3, 1, 2)

    xp = x_prompt
    xs = x_sample
    sbk_p, sbv_p, sbk_s, sbv_s = [], [], [], []
    pool_p, pool_s, conv_p, conv_s = [], [], [], []
    for l in range(DEPTH):
        last = l == DEPTH - 1
        g_mix = norm_mix_g[l].reshape(1, D_MODEL)
        g_mem = norm_mem_g[l].reshape(1, D_MODEL)
        g_ffn = norm_ffn_g[l].reshape(1, D_MODEL)
        g_fin = norm_final_g.reshape(1, D_MODEL) if last else None
        wq_sb = w_in[l, :, 0:SB_WIDTH].astype(BF16)
        wkt = w_in[l, :, SB_WIDTH:2 * SB_WIDTH].T.astype(BF16)
        wvt = w_in[l, :, 2 * SB_WIDTH:3 * SB_WIDTH].T.astype(BF16)
        wr = w_in[l, :, 3 * SB_WIDTH:].astype(BF16)
        wo, wq, wmo = (w.astype(BF16) for w in (w_out[l], w_mq[l], w_mo[l]))
        wup, wdn = w_up[l].astype(BF16), w_down[l].astype(BF16)
        wp = _block_diag(pool_w[l]).astype(BF16)
        ps = pool_scale[l].reshape(1, POOL_WIDTH)
        bdw = conv_b[l].reshape(1, CONV_WIDTH)
        lng = conv_ln_g[l].reshape(1, CONV_WIDTH)
        lnb = conv_ln_b[l].reshape(1, CONV_WIDTH)
        pw = conv_pw[l].astype(BF16)
        bias_rows = jnp.broadcast_to(jnp.tile(sb_bias[l], n_pages)[:, None], (n_pages * SB_HEADS, LANES))

        q, kt, vt, ktb, vtb, u, c = _in_proj(xp.reshape(bp, seq, D_MODEL), g_mix, wq_sb, wkt, wvt, wr)
        sbk_p.append(heads_last(kt))
        sbv_p.append(heads_last(vt))
        pool_p.append(u[:, seq - POOL_STATE:])
        conv_p.append(c[:, seq - CONV_STATE:])
        a = _sb_prompt(sb_bias[l], q, ktb, vtb, tri_prompt)
        bc = _pool_conv_prompt(u, c, wp, ps, conv_w[l], bdw, lng, lnb, pw)
        x1, qm = _out_proj(xp.reshape(bp * seq, D_MODEL), a.reshape(bp * seq, SB_WIDTH),
                           bc.reshape(bp * seq, POOL_WIDTH + CONV_WIDTH), wo, g_mem, wq)
        o = _mem_attn_prompt(l, qm.reshape(bp, seq, D_MODEL), mkb_all, mvb_all)
        xp = _mo_ffn(x1, o.reshape(bp * seq, D_MODEL), wmo, g_ffn, wup, wdn, g_fin)

        q, kt, vt, _, _, u, c = _in_proj(xs.reshape(1, bs, D_MODEL), g_mix, wq_sb, wkt, wvt, wr)
        sbk_s.append(heads_last(kt).reshape(bs, 1, SB_HEADS, SB_HEAD_DIM))
        sbv_s.append(heads_last(vt).reshape(bs, 1, SB_HEADS, SB_HEAD_DIM))
        u, c = u[0], c[0]
        pool_s.append(jnp.concatenate([state_pool[l, :, 1:], u[:, None]], axis=1))
        conv_s.append(jnp.concatenate([state_conv[l, :, 1:], c[:, None]], axis=1))
        a = _sb_sample(l, page_table, q.reshape(bs, 1, SB_WIDTH), bias_rows, tri_page, pages_mat,
                       cache_k, cache_v)
        bc = _pool_conv_sample(l, u, c, pool_rows, conv_rows, pwin, wp, ps, conv_w[l], bdw, lng, lnb, pw)
        x1, qm = _out_proj(xs.reshape(bs, D_MODEL), a.reshape(bs, SB_WIDTH), bc, wo, g_mem, wq)
        o = _mem_attn_sample(l, qm.reshape(bs, 1, D_MODEL), cmem_k, cmem_v)
        xs = _mo_ffn(x1, o.reshape(bs, D_MODEL), wmo, g_ffn, wup, wdn, g_fin)

    return (xp.reshape(bp, seq, D_MODEL), xs.reshape(bs, 1, D_MODEL),
            jnp.stack(sbk_p), jnp.stack(sbv_p), jnp.stack(sbk_s), jnp.stack(sbv_s),
            mk_all.reshape(DEPTH, bp, mem_len, MEM_HEADS, MEM_HEAD_DIM),
            mv_all.reshape(DEPTH, bp, mem_len, MEM_HEADS, MEM_HEAD_DIM),
            jnp.stack(pool_p), jnp.stack(pool_s), jnp.stack(conv_p), jnp.stack(conv_s))
```

```python
import functools

import jax
import jax.numpy as jnp
from jax import lax
from jax.experimental import pallas as pl
from jax.experimental.pallas import tpu as pltpu

F32 = jnp.float32
BF16 = jnp.bfloat16

D_MODEL = 1024
DEPTH = 2
PAGE_SIZE = 128
SB_HEAD_DIM = 64
SB_WIDTH = D_MODEL // 2
SB_HEADS = SB_WIDTH // SB_HEAD_DIM
POOL_WINDOWS = (2, 4, 8, 16)
POOL_WIDTH = D_MODEL // 4
POOL_GROUP = POOL_WIDTH // len(POOL_WINDOWS)
POOL_STATE = max(POOL_WINDOWS) - 1
CONV_WIDTH = D_MODEL // 4
CONV_K = 31
CONV_STATE = CONV_K - 1
MEM_HEADS = 4
MEM_HEAD_DIM = D_MODEL // MEM_HEADS
D_FF = 4 * D_MODEL
EPS = 1e-6

SB_SCALE = SB_HEAD_DIM ** -0.5
LOG2E = 1.4426950408889634
MEM_SCALE = MEM_HEAD_DIM ** -0.5

LANES = 128
SUBLANES = 8
SB_TILE = 256
SB_GROUP = 8
ROW_TILE = 512
SEQ_TILE = 512
HALO = 32
CONV_CHUNK = 64
FF_CHUNK = 1024


def _const_spec(shape):
    nd = len(shape)
    return pl.BlockSpec(shape, lambda *_: (0,) * nd, pipeline_mode=pl.Buffered(1))


def _rms(x, g):
    return x * lax.rsqrt(jnp.mean(x * x, axis=-1, keepdims=True) + EPS) * g


def _softplus2(z):
    return jnp.maximum(z, 0.0) + jnp.log(1.0 + jnp.exp2(-jnp.abs(z))) * LOG2E


def _dot(a, b):
    return jnp.dot(a, b, preferred_element_type=F32)


def _dot_nt(a, b):
    return lax.dot_general(a, b, (((1,), (1,)), ((), ())), preferred_element_type=F32)


def _mem_kv_kernel(m_ref, wk_ref, wv_ref, k_ref, v_ref, kb_ref, vb_ref):
    m = m_ref[...].astype(BF16)
    k = _dot(m, wk_ref[...])
    v = _dot(m, wv_ref[...])
    k_ref[...] = k
    v_ref[...] = v
    kb_ref[...] = k.astype(BF16)
    vb_ref[...] = v.astype(BF16)


def _mem_kv(mem_rows, wk, wv):
    rows = mem_rows.shape[0]
    tm = min(ROW_TILE, rows)
    w_spec = pl.BlockSpec((None, D_MODEL, D_MODEL), lambda l, i: (l, 0, 0))
    o_spec = pl.BlockSpec((None, tm, D_MODEL), lambda l, i: (l, i, 0))
    f = jax.ShapeDtypeStruct((DEPTH, rows, D_MODEL), F32)
    b = jax.ShapeDtypeStruct((DEPTH, rows, D_MODEL), BF16)
    return pl.pallas_call(
        _mem_kv_kernel,
        grid=(DEPTH, rows // tm),
        in_specs=[pl.BlockSpec((tm, D_MODEL), lambda l, i: (i, 0)), w_spec, w_spec],
        out_specs=[o_spec, o_spec, o_spec, o_spec],
        out_shape=[f, f, b, b],
        name="mem_kv",
    )(mem_rows, wk, wv)


def _in_proj_kernel(x_ref, g_ref, wq_ref, wkt_ref, wvt_ref, wr_ref, q_ref, kt_ref, vt_ref, ktb_ref, vtb_ref,
                    u_ref, c_ref):
    h = _rms(x_ref[...], g_ref[...]).astype(BF16)
    q_ref[...] = (_dot(h, wq_ref[...]) * (SB_SCALE * LOG2E)).astype(BF16)
    kt = _dot_nt(wkt_ref[...], h)
    kt_ref[...] = kt
    ktb_ref[...] = kt.astype(BF16)
    vt = _dot_nt(wvt_ref[...], h)
    vt_ref[...] = vt
    vtb_ref[...] = vt.astype(BF16)
    u_ref[...] = _dot(h, wr_ref[:, 0:POOL_WIDTH])
    glu_a = _dot(h, wr_ref[:, POOL_WIDTH:POOL_WIDTH + CONV_WIDTH])
    glu_g = _dot(h, wr_ref[:, POOL_WIDTH + CONV_WIDTH:])
    c_ref[...] = glu_a * jax.nn.sigmoid(glu_g)


def _in_proj(x, g, wq, wkt, wvt, wr):
    b, seq, _ = x.shape
    tm = min(ROW_TILE, seq)

    def row_spec(width):
        return pl.BlockSpec((None, tm, width), lambda bi, i: (bi, i, 0))

    t_spec = pl.BlockSpec((None, SB_WIDTH, tm), lambda bi, i: (bi, 0, i))

    def rows(width, dtype):
        return jax.ShapeDtypeStruct((b, seq, width), dtype)

    def cols(dtype):
        return jax.ShapeDtypeStruct((b, SB_WIDTH, seq), dtype)

    return pl.pallas_call(
        _in_proj_kernel,
        grid=(b, seq // tm),
        in_specs=[row_spec(D_MODEL), _const_spec((1, D_MODEL)), _const_spec(wq.shape), _const_spec(wkt.shape),
                  _const_spec(wvt.shape), _const_spec(wr.shape)],
        out_specs=[row_spec(SB_WIDTH), t_spec, t_spec, t_spec, t_spec, row_spec(POOL_WIDTH),
                   row_spec(CONV_WIDTH)],
        out_shape=[rows(SB_WIDTH, BF16), cols(F32), cols(F32), cols(BF16), cols(BF16), rows(POOL_WIDTH, F32),
                   rows(CONV_WIDTH, F32)],
        name="in_proj",
    )(x, g, wq, wkt, wvt, wr)


def _sb_prompt_kernel(bias_ref, q_ref, k_ref, v_ref, tri_ref, o_ref, acc_ref, run_ref, sp_ref, lsz_ref):
    t = SB_TILE
    g0 = pl.program_id(1) * SB_GROUP
    i = pl.program_id(2)
    first_head = lax.broadcasted_iota(jnp.int32, (t, LANES), 1) < SB_HEAD_DIM
    q_heads = []
    for p in range(SB_GROUP // 2):
        qp = q_ref[0, :, p * LANES:(p + 1) * LANES].astype(F32)
        q_heads += [jnp.where(first_head, qp, 0.0).astype(BF16), jnp.where(first_head, 0.0, qp).astype(BF16)]
    visible = (lax.broadcasted_iota(jnp.int32, (t, t), 1) < lax.broadcasted_iota(jnp.int32, (t, t), 0))
    acc_ref[...] = jnp.zeros_like(acc_ref)
    run_ref[...] = jnp.zeros_like(run_ref)

    def pair_rows(h):
        return slice((h // 2) * LANES, (h // 2 + 1) * LANES)

    def scores(j, masked):
        start = pl.multiple_of(j * t, t)
        for h in range(SB_GROUP):
            z = _dot(q_heads[h], k_ref[0, pair_rows(h), pl.ds(start, t)]) + bias_ref[g0 + h]
            sp = _softplus2(z)
            lsz = z - sp
            if masked:
                sp = jnp.where(visible, sp, 0.0)
                lsz = jnp.where(visible, lsz, -jnp.inf)
            sp_ref[h] = sp.astype(BF16)
            lsz_ref[h] = lsz

    def weights(j):
        start = pl.multiple_of(j * t, t)
        for h in range(SB_GROUP):
            sp = sp_ref[h]
            later = _dot(sp, tri_ref[...])
            run = run_ref[h]
            w = jnp.exp2(lsz_ref[h] + later + jnp.concatenate([run, run], axis=1))
            acc_ref[h] += _dot_nt(w.astype(BF16), v_ref[0, pair_rows(h), pl.ds(start, t)])
            run_ref[h] = run + (later[:, :1] - sp[:, :LANES].astype(F32)[:, :1])

    scores(i, True)

    def step(jj, carry):
        weights(i - jj)
        scores(i - 1 - jj, False)
        return carry

    lax.fori_loop(0, i, step, 0)
    weights(0)
    for p in range(SB_GROUP // 2):
        o_ref[0, :, p * LANES:(p + 1) * LANES] = jnp.where(first_head, acc_ref[2 * p], acc_ref[2 * p + 1]).astype(BF16)


def _sb_prompt(bias2, q, kt, vt, tri):
    b, seq, _ = q.shape
    t = SB_TILE
    width = SB_GROUP * SB_HEAD_DIM
    kv_spec = pl.BlockSpec((1, width, seq), lambda bi, g, i: (bi, g, 0))
    qo_spec = pl.BlockSpec((1, t, width), lambda bi, g, i: (bi, i, g))
    return pl.pallas_call(
        _sb_prompt_kernel,
        grid=(b, SB_HEADS // SB_GROUP, seq // t),
        in_specs=[pl.BlockSpec(memory_space=pltpu.SMEM), qo_spec, kv_spec, kv_spec, _const_spec((t, t))],
        out_specs=qo_spec,
        out_shape=jax.ShapeDtypeStruct((b, seq, SB_WIDTH), BF16),
        scratch_shapes=[pltpu.VMEM((SB_GROUP, t, LANES), F32), pltpu.VMEM((SB_GROUP, t, LANES), F32),
                        pltpu.VMEM((SB_GROUP, t, t), BF16), pltpu.VMEM((SB_GROUP, t, t), F32)],
        name="sb_prompt",
    )(bias2, q, kt, vt, tri)


def _sb_sample_kernel(n_pages, pt_ref, q_ref, bias_ref, tri_ref, pages_ref, *refs):
    del pt_ref
    k_refs = refs[:n_pages]
    v_refs = refs[n_pages:2 * n_pages]
    o_ref = refs[2 * n_pages]
    head_of_lane = lax.broadcasted_iota(jnp.int32, (SB_HEADS, SB_WIDTH), 1) // SB_HEAD_DIM
    own = head_of_lane == lax.broadcasted_iota(jnp.int32, (SB_HEADS, SB_WIDTH), 0)
    q = jnp.broadcast_to(q_ref[0].astype(F32), (SB_HEADS, SB_WIDTH))
    q_heads = jnp.where(own, q, 0.0).astype(BF16)
    z = jnp.concatenate([_dot(q_heads, k_refs[p][...].astype(BF16)) for p in range(n_pages)], axis=0)
    z = z + bias_ref[...]
    sp = _softplus2(z)
    sp_b = sp.astype(BF16)
    later_in_page = _dot(sp_b, tri_ref[...])
    later_pages = jnp.sum(_dot(pages_ref[...], sp_b), axis=1, keepdims=True)
    w = jnp.exp2((z - sp) + later_in_page + later_pages)
    acc = jnp.zeros((SB_HEADS, SB_WIDTH), F32)
    for p in range(n_pages):
        acc = acc + _dot_nt(w[p * SB_HEADS:(p + 1) * SB_HEADS].astype(BF16), v_refs[p][...].astype(BF16))
    o_ref[0] = jnp.sum(jnp.where(own, acc, 0.0), axis=0, keepdims=True).astype(BF16)


def _sb_sample(layer, page_table, q, bias_rows, tri, pages_mat, cache_k, cache_v):
    bs, n_pages = page_table.shape

    def page_spec(p):
        return pl.BlockSpec((None, None, SB_WIDTH, PAGE_SIZE), lambda b, pt: (layer, pt[b, p], 0, 0))

    rows = n_pages * SB_HEADS
    vec_spec = pl.BlockSpec((1, 1, SB_WIDTH), lambda b, pt: (b, 0, 0))
    grid_spec = pltpu.PrefetchScalarGridSpec(
        num_scalar_prefetch=1,
        grid=(bs,),
        in_specs=[vec_spec, _const_spec((rows, LANES)), _const_spec((PAGE_SIZE, PAGE_SIZE)),
                  _const_spec((rows, rows))]
        + [page_spec(p) for p in range(n_pages)] + [page_spec(p) for p in range(n_pages)],
        out_specs=vec_spec,
    )
    return pl.pallas_call(
        functools.partial(_sb_sample_kernel, n_pages),
        grid_spec=grid_spec,
        out_shape=jax.ShapeDtypeStruct((bs, 1, SB_WIDTH), BF16),
        name="sb_sample",
    )(page_table, q, bias_rows, tri, pages_mat, *([cache_k] * n_pages), *([cache_v] * n_pages))


def _layer_norm_silu(y, g, b):
    mu = jnp.mean(y, axis=-1, keepdims=True)
    yc = y - mu
    var = jnp.mean(yc * yc, axis=-1, keepdims=True)
    y = yc * lax.rsqrt(var + EPS) * g + b
    return y * jax.nn.sigmoid(y)


def _pool_conv_prompt_kernel(u_ref, c_ref, wp_ref, ps_ref, wdw_ref, bdw_ref, lng_ref, lnb_ref, pw_ref,
                             o_ref, us, cs, ys, part, zs):
    tl = SEQ_TILE
    ti = pl.program_id(1)

    @pl.when(ti == 0)
    def _():
        us[0:HALO] = jnp.zeros((HALO, POOL_WIDTH), F32)
        cs[0:HALO] = jnp.zeros((HALO, CONV_WIDTH), F32)

    @pl.when(ti > 0)
    def _():
        us[0:HALO] = us[tl:tl + HALO]
        cs[0:HALO] = cs[tl:tl + HALO]

    u = u_ref[0]
    us[HALO:HALO + tl] = u
    cs[HALO:HALO + tl] = c_ref[0]

    end = HALO + tl
    part[0, 8:end] = us[8:end] + us[7:end - 1]
    part[1, 16:end] = part[0, 16:end] + part[0, 14:end - 2]
    part[2, 24:end] = part[1, 24:end] + part[1, 20:end - 4]
    s2 = part[0, HALO:end, 0:LANES]
    s4 = part[1, HALO:end, 0:LANES]
    s8 = part[2, HALO:end, LANES:]
    s16 = s8 + part[2, HALO - 8:end - 8, LANES:]
    low_group = lax.broadcasted_iota(jnp.int32, (tl, LANES), 1) < POOL_GROUP
    seen = ti * tl + lax.broadcasted_iota(jnp.int32, (tl, LANES), 0) + 1
    cnt_lo = jnp.minimum(jnp.where(low_group, POOL_WINDOWS[0], POOL_WINDOWS[1]), seen).astype(F32)
    cnt_hi = jnp.minimum(jnp.where(low_group, POOL_WINDOWS[2], POOL_WINDOWS[3]), seen).astype(F32)
    mean = jnp.concatenate([jnp.where(low_group, s2, s4) / cnt_lo,
                            jnp.where(low_group, s8, s16) / cnt_hi], axis=1)
    pool = _dot((mean - u).astype(BF16), wp_ref[...]) * ps_ref[...]

    lead = HALO - CONV_STATE
    for r0 in range(0, tl, CONV_CHUNK):
        acc = jnp.broadcast_to(bdw_ref[...], (CONV_CHUNK, CONV_WIDTH))
        for phase in range(SUBLANES):
            rows = CONV_CHUNK if phase == 0 else CONV_CHUNK + SUBLANES
            z = None
            for k in range(CONV_K):
                if (k + lead) % SUBLANES == phase:
                    term = cs[pl.ds(r0 + k + lead - phase, rows), :] * wdw_ref[k:k + 1, :]
                    z = term if z is None else z + term
            if phase == 0:
                acc = acc + z
            else:
                zs[phase - 1] = z
                acc = acc + zs[phase - 1, pl.ds(phase, CONV_CHUNK), :]
        ys[r0:r0 + CONV_CHUNK] = acc
    conv = _dot(_layer_norm_silu(ys[...], lng_ref[...], lnb_ref[...]).astype(BF16), pw_ref[...])
    o_ref[0] = jnp.concatenate([pool, conv], axis=1).astype(BF16)


def _pool_conv_prompt(u, c, wp, ps, wdw, bdw, lng, lnb, pw):
    b, seq, _ = u.shape
    tl = SEQ_TILE
    vec = _const_spec((1, CONV_WIDTH))
    return pl.pallas_call(
        _pool_conv_prompt_kernel,
        grid=(b, seq // tl),
        in_specs=[pl.BlockSpec((1, tl, POOL_WIDTH), lambda bi, i: (bi, i, 0)),
                  pl.BlockSpec((1, tl, CONV_WIDTH), lambda bi, i: (bi, i, 0)),
                  _const_spec(wp.shape), vec, _const_spec(wdw.shape), vec, vec, vec, _const_spec(pw.shape)],
        out_specs=pl.BlockSpec((1, tl, POOL_WIDTH + CONV_WIDTH), lambda bi, i: (bi, i, 0)),
        out_shape=jax.ShapeDtypeStruct((b, seq, POOL_WIDTH + CONV_WIDTH), BF16),
        scratch_shapes=[pltpu.VMEM((HALO + tl, POOL_WIDTH), F32), pltpu.VMEM((HALO + tl, CONV_WIDTH), F32),
                        pltpu.VMEM((tl, CONV_WIDTH), F32), pltpu.VMEM((3, HALO + tl, POOL_WIDTH), F32),
                        pltpu.VMEM((SUBLANES - 1, CONV_CHUNK + SUBLANES, CONV_WIDTH), F32)],
        compiler_params=pltpu.CompilerParams(dimension_semantics=("arbitrary", "arbitrary")),
        name="pool_conv_prompt",
    )(u, c, wp, ps, wdw, bdw, lng, lnb, pw)


def _pool_conv_sample_kernel(u_ref, c_ref, pprev_ref, cprev_ref, pwin_ref, wp_ref, ps_ref, wdw_ref, bdw_ref,
                             lng_ref, lnb_ref, pw_ref, o_ref):
    u = u_ref[...]
    c = c_ref[...]
    mean = u * pwin_ref[POOL_STATE:POOL_STATE + 1, :]
    for k in range(POOL_STATE):
        mean = mean + pprev_ref[k] * pwin_ref[k:k + 1, :]
    pool = _dot((mean - u).astype(BF16), wp_ref[...]) * ps_ref[...]
    y = c * wdw_ref[CONV_STATE:CONV_K, :] + bdw_ref[...]
    for k in range(CONV_STATE):
        y = y + cprev_ref[k] * wdw_ref[k:k + 1, :]
    conv = _dot(_layer_norm_silu(y, lng_ref[...], lnb_ref[...]).astype(BF16), pw_ref[...])
    o_ref[...] = jnp.concatenate([pool, conv], axis=1).astype(BF16)


def _pool_conv_sample(layer, u, c, pool_rows, conv_rows, pwin, wp, ps, wdw, bdw, lng, lnb, pw):
    bs = u.shape[0]
    tb = 32
    vec = _const_spec((1, CONV_WIDTH))
    return pl.pallas_call(
        _pool_conv_sample_kernel,
        grid=(bs // tb,),
        in_specs=[pl.BlockSpec((tb, POOL_WIDTH), lambda i: (i, 0)),
                  pl.BlockSpec((tb, CONV_WIDTH), lambda i: (i, 0)),
                  pl.BlockSpec((None, POOL_STATE, tb, POOL_WIDTH), lambda i: (layer, 0, i, 0)),
                  pl.BlockSpec((None, CONV_STATE, tb, CONV_WIDTH), lambda i: (layer, 0, i, 0)),
                  _const_spec(pwin.shape), _const_spec(wp.shape), vec, _const_spec(wdw.shape), vec, vec, vec,
                  _const_spec(pw.shape)],
        out_specs=pl.BlockSpec((tb, POOL_WIDTH + CONV_WIDTH), lambda i: (i, 0)),
        out_shape=jax.ShapeDtypeStruct((bs, POOL_WIDTH + CONV_WIDTH), BF16),
        name="pool_conv_sample",
    )(u, c, pool_rows, conv_rows, pwin, wp, ps, wdw, bdw, lng, lnb, pw)


def _out_proj_kernel(x_ref, a_ref, bc_ref, wo_ref, g_ref, wq_ref, x1_ref, q_ref):
    x1 = x_ref[...] + _dot(a_ref[...], wo_ref[0:SB_WIDTH, :]) + _dot(bc_ref[...], wo_ref[SB_WIDTH:, :])
    x1_ref[...] = x1
    h = _rms(x1, g_ref[...]).astype(BF16)
    q_ref[...] = (_dot(h, wq_ref[...]) * MEM_SCALE).astype(BF16)


def _out_proj(x, a, bc, wo, g, wq):
    rows = x.shape[0]
    tm = min(ROW_TILE, rows)

    def row_spec(width):
        return pl.BlockSpec((tm, width), lambda i: (i, 0))

    return pl.pallas_call(
        _out_proj_kernel,
        grid=(rows // tm,),
        in_specs=[row_spec(D_MODEL), row_spec(SB_WIDTH), row_spec(POOL_WIDTH + CONV_WIDTH),
                  _const_spec(wo.shape), _const_spec((1, D_MODEL)), _const_spec(wq.shape)],
        out_specs=[row_spec(D_MODEL), row_spec(D_MODEL)],
        out_shape=[jax.ShapeDtypeStruct((rows, D_MODEL), F32), jax.ShapeDtypeStruct((rows, D_MODEL), BF16)],
        name="out_proj",
    )(x, a, bc, wo, g, wq)


def _mem_attn_prompt_kernel(q_ref, mk_ref, mv_ref, o_ref):
    outs = []
    for h in range(MEM_HEADS):
        hs = slice(h * MEM_HEAD_DIM, (h + 1) * MEM_HEAD_DIM)
        s = _dot_nt(q_ref[0, :, hs], mk_ref[:, hs])
        e = jnp.exp(s - jnp.max(s, axis=-1, keepdims=True))
        den = jnp.sum(e, axis=-1, keepdims=True)
        outs.append(_dot((e / den).astype(BF16), mv_ref[:, hs]))
    o_ref[0] = jnp.concatenate(outs, axis=1).astype(BF16)


def _mem_attn_prompt(layer, q, mk, mv):
    b, seq, _ = q.shape
    tm = ROW_TILE
    mem_len = mk.shape[2]
    m_spec = pl.BlockSpec((None, None, mem_len, D_MODEL), lambda bi, i: (layer, bi, 0, 0))
    qo_spec = pl.BlockSpec((1, tm, D_MODEL), lambda bi, i: (bi, i, 0))
    return pl.pallas_call(
        _mem_attn_prompt_kernel,
        grid=(b, seq // tm),
        in_specs=[qo_spec, m_spec, m_spec],
        out_specs=qo_spec,
        out_shape=jax.ShapeDtypeStruct((b, seq, D_MODEL), BF16),
        name="mem_attn_prompt",
    )(q, mk, mv)


def _mem_attn_sample_kernel(q_ref, mk_ref, mv_ref, o_ref):
    q = q_ref[0].astype(F32)
    s = jnp.sum(mk_ref[...] * q[None], axis=-1, keepdims=True)
    e = jnp.exp(s - jnp.max(s, axis=0, keepdims=True))
    p = e / jnp.sum(e, axis=0, keepdims=True)
    o_ref[0] = jnp.sum(p * mv_ref[...], axis=0).astype(BF16)


def _mem_attn_sample(layer, q, cache_k, cache_v):
    bs = q.shape[0]
    mem_len = cache_k.shape[2]
    m_spec = pl.BlockSpec((None, None, mem_len, MEM_HEADS, MEM_HEAD_DIM), lambda b: (layer, b, 0, 0, 0))
    vec_spec = pl.BlockSpec((1, MEM_HEADS, MEM_HEAD_DIM), lambda b: (b, 0, 0))
    return pl.pallas_call(
        _mem_attn_sample_kernel,
        grid=(bs,),
        in_specs=[vec_spec, m_spec, m_spec],
        out_specs=vec_spec,
        out_shape=jax.ShapeDtypeStruct((bs, MEM_HEADS, MEM_HEAD_DIM), BF16),
        name="mem_attn_sample",
    )(q, cache_k, cache_v)


def _mo_ffn_kernel(final, x1_ref, o_ref, wmo_ref, g_ref, wup_ref, wdn_ref, *refs):
    out_ref = refs[-1]
    x2 = x1_ref[...] + _dot(o_ref[...], wmo_ref[...])
    h = _rms(x2, g_ref[...]).astype(BF16)
    acc = x2
    for c0 in range(0, D_FF, FF_CHUNK):
        up = jnp.maximum(_dot(h, wup_ref[:, c0:c0 + FF_CHUNK]), 0.0)
        acc = acc + _dot((up * up).astype(BF16), wdn_ref[c0:c0 + FF_CHUNK, :])
    if final:
        acc = _rms(acc, refs[0][...])
    out_ref[...] = acc


def _mo_ffn(x1, o, wmo, g, wup, wdn, g_final=None):
    rows = x1.shape[0]
    tm = min(ROW_TILE, rows)
    row_spec = pl.BlockSpec((tm, D_MODEL), lambda i: (i, 0))
    final = g_final is not None
    in_specs = [row_spec, row_spec, _const_spec(wmo.shape), _const_spec((1, D_MODEL)), _const_spec(wup.shape),
                _const_spec(wdn.shape)]
    args = [x1, o, wmo, g, wup, wdn]
    if final:
        in_specs.append(_const_spec((1, D_MODEL)))
        args.append(g_final)
    return pl.pallas_call(
        functools.partial(_mo_ffn_kernel, final),
        grid=(rows // tm,),
        in_specs=in_specs,
        out_specs=row_spec,
        out_shape=jax.ShapeDtypeStruct((rows, D_MODEL), F32),
        name="mo_ffn",
    )(*args)


def _suffix_triangle(n):
    j = lax.broadcasted_iota(jnp.int32, (n, n), 0)
    s = lax.broadcasted_iota(jnp.int32, (n, n), 1)
    return jnp.where(j > s, -1.0, 0.0).astype(BF16)


def _later_pages_matrix(n_pages):
    n = n_pages * SB_HEADS
    r = lax.broadcasted_iota(jnp.int32, (n, n), 0)
    c = lax.broadcasted_iota(jnp.int32, (n, n), 1)
    same_head = (r % SB_HEADS) == (c % SB_HEADS)
    return jnp.where(same_head & (c // SB_HEADS > r // SB_HEADS), -1.0, 0.0).astype(BF16)


def _pool_window_rows():
    row = lax.broadcasted_iota(jnp.int32, (POOL_STATE + 1, POOL_WIDTH), 0)
    ch = lax.broadcasted_iota(jnp.int32, (POOL_STATE + 1, POOL_WIDTH), 1)
    w = jnp.zeros((POOL_STATE + 1, POOL_WIDTH), jnp.int32)
    for g, win in enumerate(POOL_WINDOWS):
        w = jnp.where(ch // POOL_GROUP == g, win, w)
    return jnp.where(row >= POOL_STATE + 1 - w, 1.0 / w.astype(F32), 0.0)


def _block_diag(w_grp):
    n, g, _ = w_grp.shape
    out = jnp.zeros((n * g, n * g), w_grp.dtype)
    for i in range(n):
        out = lax.dynamic_update_slice(out, w_grp[i], (i * g, i * g))
    return out


def kernel(x_prompt, x_sample, cache_sb_k, cache_sb_v, cache_mem_k, cache_mem_v, state_pool, state_conv,
           page_table, mem_prompt, norm_mix_g, w_in, sb_bias, pool_w, pool_scale, conv_w, conv_b, conv_ln_g,
           conv_ln_b, conv_pw, w_out, norm_mem_g, w_mq, w_mk, w_mv, w_mo, norm_ffn_g, w_up, w_down,
           norm_final_g):
    bp, seq, _ = x_prompt.shape
    bs = x_sample.shape[0]
    n_pages = page_table.shape[1]
    n_phys = cache_sb_k.shape[1]
    mem_len = mem_prompt.shape[1]
    assert x_sample.shape[1] == 1 and seq % SB_TILE == 0 and seq >= CONV_STATE

    cache_k = cache_sb_k.transpose(0, 1, 3, 4, 2).reshape(DEPTH, n_phys, SB_WIDTH, PAGE_SIZE)
    cache_v = cache_sb_v.transpose(0, 1, 3, 4, 2).reshape(DEPTH, n_phys, SB_WIDTH, PAGE_SIZE)
    pool_rows = state_pool.transpose(0, 2, 1, 3)
    conv_rows = state_conv.transpose(0, 2, 1, 3)

    tri_prompt = _suffix_triangle(SB_TILE)
    tri_page = _suffix_triangle(PAGE_SIZE)
    pages_mat = _later_pages_matrix(n_pages)
    pwin = _pool_window_rows()

    mk_all, mv_all, mkb_all, mvb_all = _mem_kv(mem_prompt.reshape(bp * mem_len, D_MODEL),
                                                w_mk.astype(BF16), w_mv.astype(BF16))
    mkb_all = mkb_all.reshape(DEPTH, bp, mem_len, D_MODEL)
    mvb_all = mvb_all.reshape(DEPTH, bp, mem_len, D_MODEL)

    def heads_last(t):
        return t.reshape(t.shape[0], SB_HEADS, SB_HEAD_DIM, t.shape[2]).transpose(0, 3, 1, 2)

    xp = x_prompt
    xs = x_sample
    sbk_p, sbv_p, sbk_s, sbv_s = [], [], [], []
    pool_p, pool_s, conv_p, conv_s = [], [], [], []
    for l in range(DEPTH):
        last = l == DEPTH - 1
        g_mix = norm_mix_g[l].reshape(1, D_MODEL)
        g_mem = norm_mem_g[l].reshape(1, D_MODEL)
        g_ffn = norm_ffn_g[l].reshape(1, D_MODEL)
        g_fin = norm_final_g.reshape(1, D_MODEL) if last else None
        wq_sb = w_in[l, :, 0:SB_WIDTH].astype(BF16)
        wkt = w_in[l, :, SB_WIDTH:2 * SB_WIDTH].T.astype(BF16)
        wvt = w_in[l, :, 2 * SB_WIDTH:3 * SB_WIDTH].T.astype(BF16)
        wr = w_in[l, :, 3 * SB_WIDTH:].astype(BF16)
        wo, wq, wmo = (w.astype(BF16) for w in (w_out[l], w_mq[l], w_mo[l]))
        wup, wdn = w_up[l].astype(BF16), w_down[l].astype(BF16)
        wp = _block_diag(pool_w[l]).astype(BF16)
        ps = pool_scale[l].reshape(1, POOL_WIDTH)
        bdw = conv_b[l].reshape(1, CONV_WIDTH)
        lng = conv_ln_g[l].reshape(1, CONV_WIDTH)
        lnb = conv_ln_b[l].reshape(1, CONV_WIDTH)
        pw = conv_pw[l].astype(BF16)
        bias2 = sb_bias[l] * LOG2E
        bias_rows = jnp.broadcast_to(jnp.tile(bias2, n_pages)[:, None], (n_pages * SB_HEADS, LANES))

        q, kt, vt, ktb, vtb, u, c = _in_proj(xp.reshape(bp, seq, D_MODEL), g_mix, wq_sb, wkt, wvt, wr)
        sbk_p.append(heads_last(kt))
        sbv_p.append(heads_last(vt))
        pool_p.append(u[:, seq - POOL_STATE:])
        conv_p.append(c[:, seq - CONV_STATE:])
        a = _sb_prompt(bias2, q, ktb, vtb, tri_prompt)
        bc = _pool_conv_prompt(u, c, wp, ps, conv_w[l], bdw, lng, lnb, pw)
        x1, qm = _out_proj(xp.reshape(bp * seq, D_MODEL), a.reshape(bp * seq, SB_WIDTH),
                           bc.reshape(bp * seq, POOL_WIDTH + CONV_WIDTH), wo, g_mem, wq)
        o = _mem_attn_prompt(l, qm.reshape(bp, seq, D_MODEL), mkb_all, mvb_all)
        xp = _mo_ffn(x1, o.reshape(bp * seq, D_MODEL), wmo, g_ffn, wup, wdn, g_fin)

        q, kt, vt, _, _, u, c = _in_proj(xs.reshape(1, bs, D_MODEL), g_mix, wq_sb, wkt, wvt, wr)
        sbk_s.append(heads_last(kt).reshape(bs, 1, SB_HEADS, SB_HEAD_DIM))
        sbv_s.append(heads_last(vt).reshape(bs, 1, SB_HEADS, SB_HEAD_DIM))
        u, c = u[0], c[0]
        pool_s.append(jnp.concatenate([state_pool[l, :, 1:], u[:, None]], axis=1))
        conv_s.append(jnp.concatenate([state_conv[l, :, 1:], c[:, None]], axis=1))
        a = _sb_sample(l, page_table, q.reshape(bs, 1, SB_WIDTH), bias_rows, tri_page, pages_mat,
                       cache_k, cache_v)
        bc = _pool_conv_sample(l, u, c, pool_rows, conv_rows, pwin, wp, ps, conv_w[l], bdw, lng, lnb, pw)
        x1, qm = _out_proj(xs.reshape(bs, D_MODEL), a.reshape(bs, SB_WIDTH), bc, wo, g_mem, wq)
        o = _mem_attn_sample(l, qm.reshape(bs, MEM_HEADS, MEM_HEAD_DIM), cache_mem_k, cache_mem_v)
        xs = _mo_ffn(x1, o.reshape(bs, D_MODEL), wmo, g_ffn, wup, wdn, g_fin)

    return (xp.reshape(bp, seq, D_MODEL), xs.reshape(bs, 1, D_MODEL),
            jnp.stack(sbk_p), jnp.stack(sbv_p), jnp.stack(sbk_s), jnp.stack(sbv_s),
            mk_all.reshape(DEPTH, bp, mem_len, MEM_HEADS, MEM_HEAD_DIM),
            mv_all.reshape(DEPTH, bp, mem_len, MEM_HEADS, MEM_HEAD_DIM),
            jnp.stack(pool_p), jnp.stack(pool_s), jnp.stack(conv_p), jnp.stack(conv_s))
```

```python
import functools

import jax
import jax.numpy as jnp
from jax import lax
from jax.experimental import pallas as pl
from jax.experimental.pallas import tpu as pltpu

F32 = jnp.float32
BF16 = jnp.bfloat16

D_MODEL = 1024
DEPTH = 2
PAGE_SIZE = 128
SB_HEAD_DIM = 64
SB_WIDTH = D_MODEL // 2
SB_HEADS = SB_WIDTH // SB_HEAD_DIM
POOL_WINDOWS = (2, 4, 8, 16)
POOL_WIDTH = D_MODEL // 4
POOL_GROUP = POOL_WIDTH // len(POOL_WINDOWS)
POOL_STATE = max(POOL_WINDOWS) - 1
CONV_WIDTH = D_MODEL // 4
CONV_K = 31
CONV_STATE = CONV_K - 1
MEM_HEADS = 4
MEM_HEAD_DIM = D_MODEL // MEM_HEADS
D_FF = 4 * D_MODEL
EPS = 1e-6

SB_SCALE = SB_HEAD_DIM ** -0.5
LOG2E = 1.4426950408889634
MEM_SCALE = MEM_HEAD_DIM ** -0.5

LANES = 128
SUBLANES = 8
SB_TILE = 256
ROW_TILE = 512
SEQ_TILE = 512
HALO = 32
CONV_CHUNK = 64
FF_CHUNK = 1024


def _const_spec(shape):
    nd = len(shape)
    return pl.BlockSpec(shape, lambda *_: (0,) * nd, pipeline_mode=pl.Buffered(1))


def _rms(x, g):
    return x * lax.rsqrt(jnp.mean(x * x, axis=-1, keepdims=True) + EPS) * g


def _softplus2(z):
    return jnp.maximum(z, 0.0) + jnp.log(1.0 + jnp.exp2(-jnp.abs(z))) * LOG2E


def _dot(a, b):
    return jnp.dot(a, b, preferred_element_type=F32)


def _dot_nt(a, b):
    return lax.dot_general(a, b, (((1,), (1,)), ((), ())), preferred_element_type=F32)


def _mem_kv_kernel(m_ref, wk_ref, wv_ref, k_ref, v_ref, kb_ref, vb_ref):
    m = m_ref[...].astype(BF16)
    k = _dot(m, wk_ref[...])
    v = _dot(m, wv_ref[...])
    k_ref[...] = k
    v_ref[...] = v
    kb_ref[...] = k.astype(BF16)
    vb_ref[...] = v.astype(BF16)


def _mem_kv(mem_rows, wk, wv):
    rows = mem_rows.shape[0]
    tm = min(ROW_TILE, rows)
    w_spec = pl.BlockSpec((None, D_MODEL, D_MODEL), lambda l, i: (l, 0, 0))
    o_spec = pl.BlockSpec((None, tm, D_MODEL), lambda l, i: (l, i, 0))
    f = jax.ShapeDtypeStruct((DEPTH, rows, D_MODEL), F32)
    b = jax.ShapeDtypeStruct((DEPTH, rows, D_MODEL), BF16)
    return pl.pallas_call(
        _mem_kv_kernel,
        grid=(DEPTH, rows // tm),
        in_specs=[pl.BlockSpec((tm, D_MODEL), lambda l, i: (i, 0)), w_spec, w_spec],
        out_specs=[o_spec, o_spec, o_spec, o_spec],
        out_shape=[f, f, b, b],
        name="mem_kv",
    )(mem_rows, wk, wv)


def _in_proj_kernel(x_ref, g_ref, wq_ref, wkt_ref, wvt_ref, wr_ref, q_ref, kt_ref, vt_ref, ktb_ref, vtb_ref,
                    u_ref, c_ref):
    h = _rms(x_ref[...], g_ref[...]).astype(BF16)
    q_ref[...] = (_dot(h, wq_ref[...]) * (SB_SCALE * LOG2E)).astype(BF16)
    kt = _dot_nt(wkt_ref[...], h)
    kt_ref[...] = kt
    ktb_ref[...] = kt.astype(BF16)
    vt = _dot_nt(wvt_ref[...], h)
    vt_ref[...] = vt
    vtb_ref[...] = vt.astype(BF16)
    u_ref[...] = _dot(h, wr_ref[:, 0:POOL_WIDTH])
    glu_a = _dot(h, wr_ref[:, POOL_WIDTH:POOL_WIDTH + CONV_WIDTH])
    glu_g = _dot(h, wr_ref[:, POOL_WIDTH + CONV_WIDTH:])
    c_ref[...] = glu_a * jax.nn.sigmoid(glu_g)


def _in_proj(x, g, wq, wkt, wvt, wr):
    b, seq, _ = x.shape
    tm = min(ROW_TILE, seq)

    def row_spec(width):
        return pl.BlockSpec((None, tm, width), lambda bi, i: (bi, i, 0))

    t_spec = pl.BlockSpec((None, SB_WIDTH, tm), lambda bi, i: (bi, 0, i))

    def rows(width, dtype):
        return jax.ShapeDtypeStruct((b, seq, width), dtype)

    def cols(dtype):
        return jax.ShapeDtypeStruct((b, SB_WIDTH, seq), dtype)

    return pl.pallas_call(
        _in_proj_kernel,
        grid=(b, seq // tm),
        in_specs=[row_spec(D_MODEL), _const_spec((1, D_MODEL)), _const_spec(wq.shape), _const_spec(wkt.shape),
                  _const_spec(wvt.shape), _const_spec(wr.shape)],
        out_specs=[row_spec(SB_WIDTH), t_spec, t_spec, t_spec, t_spec, row_spec(POOL_WIDTH),
                   row_spec(CONV_WIDTH)],
        out_shape=[rows(SB_WIDTH, BF16), cols(F32), cols(F32), cols(BF16), cols(BF16), rows(POOL_WIDTH, F32),
                   rows(CONV_WIDTH, F32)],
        name="in_proj",
    )(x, g, wq, wkt, wvt, wr)


def _sb_sample_rows(q_rows, brow_ref, tpage_ref, pages_ref, k_refs, v_refs):
    n_pages = len(k_refs[0])
    rows = n_pages * SB_HEADS
    head_of_lane = lax.broadcasted_iota(jnp.int32, (SB_HEADS, SB_WIDTH), 1) // SB_HEAD_DIM
    own = head_of_lane == lax.broadcasted_iota(jnp.int32, (SB_HEADS, SB_WIDTH), 0)
    zs = []
    for q_row, pages in zip(q_rows, k_refs):
        q = jnp.broadcast_to(q_row.astype(F32), (SB_HEADS, SB_WIDTH))
        q_heads = jnp.where(own, q, 0.0).astype(BF16)
        zs += [_dot(q_heads, page[...].astype(BF16)) + brow_ref[0:SB_HEADS, :] for page in pages]
    z = jnp.concatenate(zs, axis=0)
    sp = _softplus2(z)
    sp_b = sp.astype(BF16)
    later_in_page = _dot(sp_b, tpage_ref[...])
    later_pages = jnp.concatenate(
        [jnp.sum(_dot(pages_ref[...], sp_b[r * rows:(r + 1) * rows]), axis=1, keepdims=True)
         for r in range(len(q_rows))], axis=0)
    w = jnp.exp2((z - sp) + later_in_page + later_pages)
    outs = []
    for r, pages in enumerate(v_refs):
        acc = jnp.zeros((SB_HEADS, SB_WIDTH), F32)
        for p, page in enumerate(pages):
            lo = r * rows + p * SB_HEADS
            acc = acc + _dot_nt(w[lo:lo + SB_HEADS].astype(BF16), page[...].astype(BF16))
        outs.append(jnp.sum(jnp.where(own, acc, 0.0), axis=0, keepdims=True))
    return outs


def _sb_kernel(n_pages, per_step, pt_ref, bias_ref, q_ref, k_ref, v_ref, tri_ref, qs_ref, brow_ref, tpage_ref,
               pages_ref, *refs):
    del pt_ref
    page_refs = refs[:2 * per_step * n_pages]
    o_ref, os_ref, acc_ref, run_ref, sp_ref, lsz_ref = refs[2 * per_step * n_pages:]
    k_refs = [page_refs[2 * r * n_pages:(2 * r + 1) * n_pages] for r in range(per_step)]
    v_refs = [page_refs[(2 * r + 1) * n_pages:(2 * r + 2) * n_pages] for r in range(per_step)]
    sample_out = _sb_sample_rows([qs_ref[r] for r in range(per_step)], brow_ref, tpage_ref, pages_ref, k_refs, v_refs)
    for r in range(per_step):
        os_ref[r] = sample_out[r].astype(BF16)

    t = SB_TILE
    i = pl.program_id(1)
    first_head = lax.broadcasted_iota(jnp.int32, (t, LANES), 1) < SB_HEAD_DIM
    q_heads = []
    for p in range(SB_HEADS // 2):
        qp = q_ref[0, :, p * LANES:(p + 1) * LANES].astype(F32)
        q_heads += [jnp.where(first_head, qp, 0.0).astype(BF16), jnp.where(first_head, 0.0, qp).astype(BF16)]
    visible = (lax.broadcasted_iota(jnp.int32, (t, t), 1) < lax.broadcasted_iota(jnp.int32, (t, t), 0))
    acc_ref[...] = jnp.zeros_like(acc_ref)
    run_ref[...] = jnp.zeros_like(run_ref)

    def pair_rows(h):
        return slice((h // 2) * LANES, (h // 2 + 1) * LANES)

    def scores(j, masked):
        start = pl.multiple_of(j * t, t)
        for h in range(SB_HEADS):
            z = _dot(q_heads[h], k_ref[0, pair_rows(h), pl.ds(start, t)]) + bias_ref[h]
            sp = _softplus2(z)
            lsz = z - sp
            if masked:
                sp = jnp.where(visible, sp, 0.0)
                lsz = jnp.where(visible, lsz, -jnp.inf)
            sp_ref[h] = sp.astype(BF16)
            lsz_ref[h] = lsz

    def weights(j):
        start = pl.multiple_of(j * t, t)
        for h in range(SB_HEADS):
            sp = sp_ref[h]
            later = _dot(sp, tri_ref[...])
            run = run_ref[h]
            w = jnp.exp2(lsz_ref[h] + later + jnp.concatenate([run, run], axis=1))
            acc_ref[h] += _dot_nt(w.astype(BF16), v_ref[0, pair_rows(h), pl.ds(start, t)])
            run_ref[h] = run + (later[:, :1] - sp[:, :LANES].astype(F32)[:, :1])

    scores(i, True)

    def step(jj, carry):
        weights(i - jj)
        scores(i - 1 - jj, False)
        return carry

    lax.fori_loop(0, i, step, 0)
    weights(0)
    for p in range(SB_HEADS // 2):
        o_ref[0, :, p * LANES:(p + 1) * LANES] = jnp.where(first_head, acc_ref[2 * p], acc_ref[2 * p + 1]).astype(BF16)


def _sb_attention(layer, bias2, q, kt, vt, tri, page_table, qs, bias_rows, tri_page, pages_mat, cache_k, cache_v):
    b, seq, _ = q.shape
    t = SB_TILE
    nq = seq // t
    bs, n_pages = page_table.shape
    assert bs % (b * nq) == 0
    per_step = bs // (b * nq)
    rows = n_pages * SB_HEADS
    kv_spec = pl.BlockSpec((1, SB_WIDTH, seq), lambda bi, i, pt: (bi, 0, 0))
    qo_spec = pl.BlockSpec((1, t, SB_WIDTH), lambda bi, i, pt: (bi, i, 0))
    sv_spec = pl.BlockSpec((per_step, 1, SB_WIDTH), lambda bi, i, pt: (bi * nq + i, 0, 0))

    def page_spec(r, p):
        return pl.BlockSpec((None, None, SB_WIDTH, PAGE_SIZE),
                            lambda bi, i, pt: (layer, pt[(bi * nq + i) * per_step + r, p], 0, 0))

    page_specs, page_args = [], []
    for r in range(per_step):
        for cache in (cache_k, cache_v):
            page_specs += [page_spec(r, p) for p in range(n_pages)]
            page_args += [cache] * n_pages
    grid_spec = pltpu.PrefetchScalarGridSpec(
        num_scalar_prefetch=1,
        grid=(b, nq),
        in_specs=[pl.BlockSpec(memory_space=pltpu.SMEM), qo_spec, kv_spec, kv_spec, _const_spec((t, t)), sv_spec,
                  _const_spec((rows, LANES)), _const_spec((PAGE_SIZE, PAGE_SIZE)), _const_spec((rows, rows))]
        + page_specs,
        out_specs=[qo_spec, sv_spec],
        scratch_shapes=[pltpu.VMEM((SB_HEADS, t, LANES), F32), pltpu.VMEM((SB_HEADS, t, LANES), F32),
                        pltpu.VMEM((SB_HEADS, t, t), BF16), pltpu.VMEM((SB_HEADS, t, t), F32)],
    )
    return pl.pallas_call(
        functools.partial(_sb_kernel, n_pages, per_step),
        grid_spec=grid_spec,
        out_shape=[jax.ShapeDtypeStruct((b, seq, SB_WIDTH), BF16), jax.ShapeDtypeStruct((bs, 1, SB_WIDTH), BF16)],
        name="sb_attention",
    )(page_table, bias2, q, kt, vt, tri, qs, bias_rows, tri_page, pages_mat, *page_args)


def _layer_norm_silu(y, g, b):
    mu = jnp.mean(y, axis=-1, keepdims=True)
    yc = y - mu
    var = jnp.mean(yc * yc, axis=-1, keepdims=True)
    y = yc * lax.rsqrt(var + EPS) * g + b
    return y * jax.nn.sigmoid(y)


def _pool_conv_prompt_kernel(u_ref, c_ref, wp_ref, ps_ref, wdw_ref, bdw_ref, lng_ref, lnb_ref, pw_ref,
                             o_ref, us, cs, ys, part, zs):
    tl = SEQ_TILE
    ti = pl.program_id(1)

    @pl.when(ti == 0)
    def _():
        us[0:HALO] = jnp.zeros((HALO, POOL_WIDTH), F32)
        cs[0:HALO] = jnp.zeros((HALO, CONV_WIDTH), F32)

    @pl.when(ti > 0)
    def _():
        us[0:HALO] = us[tl:tl + HALO]
        cs[0:HALO] = cs[tl:tl + HALO]

    u = u_ref[0]
    us[HALO:HALO + tl] = u
    cs[HALO:HALO + tl] = c_ref[0]

    end = HALO + tl
    part[0, 8:end] = us[8:end] + us[7:end - 1]
    part[1, 16:end] = part[0, 16:end] + part[0, 14:end - 2]
    part[2, 24:end] = part[1, 24:end] + part[1, 20:end - 4]
    s2 = part[0, HALO:end, 0:LANES]
    s4 = part[1, HALO:end, 0:LANES]
    s8 = part[2, HALO:end, LANES:]
    s16 = s8 + part[2, HALO - 8:end - 8, LANES:]
    low_group = lax.broadcasted_iota(jnp.int32, (tl, LANES), 1) < POOL_GROUP
    seen = ti * tl + lax.broadcasted_iota(jnp.int32, (tl, LANES), 0) + 1
    cnt_lo = jnp.minimum(jnp.where(low_group, POOL_WINDOWS[0], POOL_WINDOWS[1]), seen).astype(F32)
    cnt_hi = jnp.minimum(jnp.where(low_group, POOL_WINDOWS[2], POOL_WINDOWS[3]), seen).astype(F32)
    mean = jnp.concatenate([jnp.where(low_group, s2, s4) / cnt_lo,
                            jnp.where(low_group, s8, s16) / cnt_hi], axis=1)
    pool = _dot((mean - u).astype(BF16), wp_ref[...]) * ps_ref[...]

    lead = HALO - CONV_STATE
    for r0 in range(0, tl, CONV_CHUNK):
        acc = jnp.broadcast_to(bdw_ref[...], (CONV_CHUNK, CONV_WIDTH))
        for phase in range(SUBLANES):
            rows = CONV_CHUNK if phase == 0 else CONV_CHUNK + SUBLANES
            z = None
            for k in range(CONV_K):
                if (k + lead) % SUBLANES == phase:
                    term = cs[pl.ds(r0 + k + lead - phase, rows), :] * wdw_ref[k:k + 1, :]
                    z = term if z is None else z + term
            if phase == 0:
                acc = acc + z
            else:
                zs[phase - 1] = z
                acc = acc + zs[phase - 1, pl.ds(phase, CONV_CHUNK), :]
        ys[r0:r0 + CONV_CHUNK] = acc
    conv = _dot(_layer_norm_silu(ys[...], lng_ref[...], lnb_ref[...]).astype(BF16), pw_ref[...])
    o_ref[0] = jnp.concatenate([pool, conv], axis=1).astype(BF16)


def _pool_conv_prompt(u, c, wp, ps, wdw, bdw, lng, lnb, pw):
    b, seq, _ = u.shape
    tl = SEQ_TILE
    vec = _const_spec((1, CONV_WIDTH))
    return pl.pallas_call(
        _pool_conv_prompt_kernel,
        grid=(b, seq // tl),
        in_specs=[pl.BlockSpec((1, tl, POOL_WIDTH), lambda bi, i: (bi, i, 0)),
                  pl.BlockSpec((1, tl, CONV_WIDTH), lambda bi, i: (bi, i, 0)),
                  _const_spec(wp.shape), vec, _const_spec(wdw.shape), vec, vec, vec, _const_spec(pw.shape)],
        out_specs=pl.BlockSpec((1, tl, POOL_WIDTH + CONV_WIDTH), lambda bi, i: (bi, i, 0)),
        out_shape=jax.ShapeDtypeStruct((b, seq, POOL_WIDTH + CONV_WIDTH), BF16),
        scratch_shapes=[pltpu.VMEM((HALO + tl, POOL_WIDTH), F32), pltpu.VMEM((HALO + tl, CONV_WIDTH), F32),
                        pltpu.VMEM((tl, CONV_WIDTH), F32), pltpu.VMEM((3, HALO + tl, POOL_WIDTH), F32),
                        pltpu.VMEM((SUBLANES - 1, CONV_CHUNK + SUBLANES, CONV_WIDTH), F32)],
        compiler_params=pltpu.CompilerParams(dimension_semantics=("arbitrary", "arbitrary")),
        name="pool_conv_prompt",
    )(u, c, wp, ps, wdw, bdw, lng, lnb, pw)


def _pool_conv_sample_kernel(u_ref, c_ref, pprev_ref, cprev_ref, pwin_ref, wp_ref, ps_ref, wdw_ref, bdw_ref,
                             lng_ref, lnb_ref, pw_ref, o_ref):
    u = u_ref[...]
    c = c_ref[...]
    mean = u * pwin_ref[POOL_STATE:POOL_STATE + 1, :]
    for k in range(POOL_STATE):
        mean = mean + pprev_ref[k] * pwin_ref[k:k + 1, :]
    pool = _dot((mean - u).astype(BF16), wp_ref[...]) * ps_ref[...]
    y = c * wdw_ref[CONV_STATE:CONV_K, :] + bdw_ref[...]
    for k in range(CONV_STATE):
        y = y + cprev_ref[k] * wdw_ref[k:k + 1, :]
    conv = _dot(_layer_norm_silu(y, lng_ref[...], lnb_ref[...]).astype(BF16), pw_ref[...])
    o_ref[...] = jnp.concatenate([pool, conv], axis=1).astype(BF16)


def _pool_conv_sample(layer, u, c, pool_rows, conv_rows, pwin, wp, ps, wdw, bdw, lng, lnb, pw):
    bs = u.shape[0]
    tb = 32
    vec = _const_spec((1, CONV_WIDTH))
    return pl.pallas_call(
        _pool_conv_sample_kernel,
        grid=(bs // tb,),
        in_specs=[pl.BlockSpec((tb, POOL_WIDTH), lambda i: (i, 0)),
                  pl.BlockSpec((tb, CONV_WIDTH), lambda i: (i, 0)),
                  pl.BlockSpec((None, POOL_STATE, tb, POOL_WIDTH), lambda i: (layer, 0, i, 0)),
                  pl.BlockSpec((None, CONV_STATE, tb, CONV_WIDTH), lambda i: (layer, 0, i, 0)),
                  _const_spec(pwin.shape), _const_spec(wp.shape), vec, _const_spec(wdw.shape), vec, vec, vec,
                  _const_spec(pw.shape)],
        out_specs=pl.BlockSpec((tb, POOL_WIDTH + CONV_WIDTH), lambda i: (i, 0)),
        out_shape=jax.ShapeDtypeStruct((bs, POOL_WIDTH + CONV_WIDTH), BF16),
        name="pool_conv_sample",
    )(u, c, pool_rows, conv_rows, pwin, wp, ps, wdw, bdw, lng, lnb, pw)


def _out_proj_kernel(x_ref, a_ref, bc_ref, wo_ref, g_ref, wq_ref, x1_ref, q_ref):
    x1 = x_ref[...] + _dot(a_ref[...], wo_ref[0:SB_WIDTH, :]) + _dot(bc_ref[...], wo_ref[SB_WIDTH:, :])
    x1_ref[...] = x1
    h = _rms(x1, g_ref[...]).astype(BF16)
    q_ref[...] = (_dot(h, wq_ref[...]) * MEM_SCALE).astype(BF16)


def _out_proj(x, a, bc, wo, g, wq):
    rows = x.shape[0]
    tm = min(ROW_TILE, rows)

    def row_spec(width):
        return pl.BlockSpec((tm, width), lambda i: (i, 0))

    return pl.pallas_call(
        _out_proj_kernel,
        grid=(rows // tm,),
        in_specs=[row_spec(D_MODEL), row_spec(SB_WIDTH), row_spec(POOL_WIDTH + CONV_WIDTH),
                  _const_spec(wo.shape), _const_spec((1, D_MODEL)), _const_spec(wq.shape)],
        out_specs=[row_spec(D_MODEL), row_spec(D_MODEL)],
        out_shape=[jax.ShapeDtypeStruct((rows, D_MODEL), F32), jax.ShapeDtypeStruct((rows, D_MODEL), BF16)],
        name="out_proj",
    )(x, a, bc, wo, g, wq)


def _mem_attn_prompt_kernel(q_ref, mk_ref, mv_ref, o_ref):
    outs = []
    for h in range(MEM_HEADS):
        hs = slice(h * MEM_HEAD_DIM, (h + 1) * MEM_HEAD_DIM)
        s = _dot_nt(q_ref[0, :, hs], mk_ref[:, hs])
        e = jnp.exp(s - jnp.max(s, axis=-1, keepdims=True))
        den = jnp.sum(e, axis=-1, keepdims=True)
        outs.append(_dot((e / den).astype(BF16), mv_ref[:, hs]))
    o_ref[0] = jnp.concatenate(outs, axis=1).astype(BF16)


def _mem_attn_prompt(layer, q, mk, mv):
    b, seq, _ = q.shape
    tm = ROW_TILE
    mem_len = mk.shape[2]
    m_spec = pl.BlockSpec((None, None, mem_len, D_MODEL), lambda bi, i: (layer, bi, 0, 0))
    qo_spec = pl.BlockSpec((1, tm, D_MODEL), lambda bi, i: (bi, i, 0))
    return pl.pallas_call(
        _mem_attn_prompt_kernel,
        grid=(b, seq // tm),
        in_specs=[qo_spec, m_spec, m_spec],
        out_specs=qo_spec,
        out_shape=jax.ShapeDtypeStruct((b, seq, D_MODEL), BF16),
        name="mem_attn_prompt",
    )(q, mk, mv)


def _mem_attn_one(q, mk_ref, mv_ref):
    q = q.astype(F32)
    s = jnp.sum(mk_ref[...] * q[None], axis=-1, keepdims=True)
    e = jnp.exp(s - jnp.max(s, axis=0, keepdims=True))
    p = e / jnp.sum(e, axis=0, keepdims=True)
    return jnp.sum(p * mv_ref[...], axis=0)


def _mo_ffn_kernel(final, mem_per_step, x1_ref, o_ref, wmo_ref, g_ref, wup_ref, wdn_ref, *refs):
    refs = list(refs)
    gf_ref = refs.pop(0) if final else None
    if mem_per_step:
        qs_ref, mk_ref, mv_ref = refs[:3]
        out_ref, os_ref = refs[3:]
        for r in range(mem_per_step):
            os_ref[r] = _mem_attn_one(qs_ref[r], mk_ref.at[r], mv_ref.at[r]).astype(BF16)
    else:
        (out_ref,) = refs
    x2 = x1_ref[...] + _dot(o_ref[...], wmo_ref[...])
    h = _rms(x2, g_ref[...]).astype(BF16)
    acc = x2
    for c0 in range(0, D_FF, FF_CHUNK):
        up = jnp.maximum(_dot(h, wup_ref[:, c0:c0 + FF_CHUNK]), 0.0)
        acc = acc + _dot((up * up).astype(BF16), wdn_ref[c0:c0 + FF_CHUNK, :])
    if final:
        acc = _rms(acc, gf_ref[...])
    out_ref[...] = acc


def _mo_ffn(x1, o, wmo, g, wup, wdn, g_final=None, mem=None):
    rows = x1.shape[0]
    tm = min(ROW_TILE, rows)
    steps = rows // tm
    row_spec = pl.BlockSpec((tm, D_MODEL), lambda i: (i, 0))
    final = g_final is not None
    in_specs = [row_spec, row_spec, _const_spec(wmo.shape), _const_spec((1, D_MODEL)), _const_spec(wup.shape),
                _const_spec(wdn.shape)]
    args = [x1, o, wmo, g, wup, wdn]
    out_specs = [row_spec]
    out_shape = [jax.ShapeDtypeStruct((rows, D_MODEL), F32)]
    if final:
        in_specs.append(_const_spec((1, D_MODEL)))
        args.append(g_final)
    per_step = 0
    if mem is not None:
        layer, qs, cache_k, cache_v = mem
        bs, mem_len = qs.shape[0], cache_k.shape[2]
        assert bs % steps == 0
        per_step = bs // steps
        vec_spec = pl.BlockSpec((per_step, MEM_HEADS, MEM_HEAD_DIM), lambda i: (i, 0, 0))
        m_spec = pl.BlockSpec((None, per_step, mem_len, MEM_HEADS, MEM_HEAD_DIM), lambda i: (layer, i, 0, 0, 0))
        in_specs += [vec_spec, m_spec, m_spec]
        args += [qs, cache_k, cache_v]
        out_specs.append(vec_spec)
        out_shape.append(jax.ShapeDtypeStruct((bs, MEM_HEADS, MEM_HEAD_DIM), BF16))
    outs = pl.pallas_call(
        functools.partial(_mo_ffn_kernel, final, per_step),
        grid=(steps,),
        in_specs=in_specs,
        out_specs=out_specs,
        out_shape=out_shape,
        name="mo_ffn",
    )(*args)
    return outs if mem is not None else outs[0]


def _suffix_triangle(n):
    j = lax.broadcasted_iota(jnp.int32, (n, n), 0)
    s = lax.broadcasted_iota(jnp.int32, (n, n), 1)
    return jnp.where(j > s, -1.0, 0.0).astype(BF16)


def _later_pages_matrix(n_pages):
    n = n_pages * SB_HEADS
    r = lax.broadcasted_iota(jnp.int32, (n, n), 0)
    c = lax.broadcasted_iota(jnp.int32, (n, n), 1)
    same_head = (r % SB_HEADS) == (c % SB_HEADS)
    return jnp.where(same_head & (c // SB_HEADS > r // SB_HEADS), -1.0, 0.0).astype(BF16)


def _pool_window_rows():
    row = lax.broadcasted_iota(jnp.int32, (POOL_STATE + 1, POOL_WIDTH), 0)
    ch = lax.broadcasted_iota(jnp.int32, (POOL_STATE + 1, POOL_WIDTH), 1)
    w = jnp.zeros((POOL_STATE + 1, POOL_WIDTH), jnp.int32)
    for g, win in enumerate(POOL_WINDOWS):
        w = jnp.where(ch // POOL_GROUP == g, win, w)
    return jnp.where(row >= POOL_STATE + 1 - w, 1.0 / w.astype(F32), 0.0)


def _block_diag(w_grp):
    n, g, _ = w_grp.shape
    out = jnp.zeros((n * g, n * g), w_grp.dtype)
    for i in range(n):
        out = lax.dynamic_update_slice(out, w_grp[i], (i * g, i * g))
    return out


def kernel(x_prompt, x_sample, cache_sb_k, cache_sb_v, cache_mem_k, cache_mem_v, state_pool, state_conv,
           page_table, mem_prompt, norm_mix_g, w_in, sb_bias, pool_w, pool_scale, conv_w, conv_b, conv_ln_g,
           conv_ln_b, conv_pw, w_out, norm_mem_g, w_mq, w_mk, w_mv, w_mo, norm_ffn_g, w_up, w_down,
           norm_final_g):
    bp, seq, _ = x_prompt.shape
    bs = x_sample.shape[0]
    n_pages = page_table.shape[1]
    n_phys = cache_sb_k.shape[1]
    mem_len = mem_prompt.shape[1]
    assert x_sample.shape[1] == 1 and seq % SB_TILE == 0 and seq >= CONV_STATE

    cache_k = cache_sb_k.transpose(0, 1, 3, 4, 2).reshape(DEPTH, n_phys, SB_WIDTH, PAGE_SIZE)
    cache_v = cache_sb_v.transpose(0, 1, 3, 4, 2).reshape(DEPTH, n_phys, SB_WIDTH, PAGE_SIZE)
    pool_rows = state_pool.transpose(0, 2, 1, 3)
    conv_rows = state_conv.transpose(0, 2, 1, 3)

    tri_prompt = _suffix_triangle(SB_TILE)
    tri_page = _suffix_triangle(PAGE_SIZE)
    pages_mat = _later_pages_matrix(n_pages)
    pwin = _pool_window_rows()

    mk_all, mv_all, mkb_all, mvb_all = _mem_kv(mem_prompt.reshape(bp * mem_len, D_MODEL),
                                                w_mk.astype(BF16), w_mv.astype(BF16))
    mkb_all = mkb_all.reshape(DEPTH, bp, mem_len, D_MODEL)
    mvb_all = mvb_all.reshape(DEPTH, bp, mem_len, D_MODEL)

    def heads_last(t):
        return t.reshape(t.shape[0], SB_HEADS, SB_HEAD_DIM, t.shape[2]).transpose(0, 3, 1, 2)

    xp = x_prompt
    xs = x_sample
    sbk_p, sbv_p, sbk_s, sbv_s = [], [], [], []
    pool_p, pool_s, conv_p, conv_s = [], [], [], []
    for l in range(DEPTH):
        last = l == DEPTH - 1
        g_mix = norm_mix_g[l].reshape(1, D_MODEL)
        g_mem = norm_mem_g[l].reshape(1, D_MODEL)
        g_ffn = norm_ffn_g[l].reshape(1, D_MODEL)
        g_fin = norm_final_g.reshape(1, D_MODEL) if last else None
        wq_sb = w_in[l, :, 0:SB_WIDTH].astype(BF16)
        wkt = w_in[l, :, SB_WIDTH:2 * SB_WIDTH].T.astype(BF16)
        wvt = w_in[l, :, 2 * SB_WIDTH:3 * SB_WIDTH].T.astype(BF16)
        wr = w_in[l, :, 3 * SB_WIDTH:].astype(BF16)
        wo, wq, wmo = (w.astype(BF16) for w in (w_out[l], w_mq[l], w_mo[l]))
        wup, wdn = w_up[l].astype(BF16), w_down[l].astype(BF16)
        wp = _block_diag(pool_w[l]).astype(BF16)
        ps = pool_scale[l].reshape(1, POOL_WIDTH)
        bdw = conv_b[l].reshape(1, CONV_WIDTH)
        lng = conv_ln_g[l].reshape(1, CONV_WIDTH)
        lnb = conv_ln_b[l].reshape(1, CONV_WIDTH)
        pw = conv_pw[l].astype(BF16)
        bias2 = sb_bias[l] * LOG2E
        bias_rows = jnp.broadcast_to(jnp.tile(bias2, n_pages)[:, None], (n_pages * SB_HEADS, LANES))

        q, kt, vt, ktb, vtb, u, c = _in_proj(xp.reshape(bp, seq, D_MODEL), g_mix, wq_sb, wkt, wvt, wr)
        sbk_p.append(heads_last(kt))
        sbv_p.append(heads_last(vt))
        pool_p.append(u[:, seq - POOL_STATE:])
        conv_p.append(c[:, seq - CONV_STATE:])
        q_s, kt_s, vt_s, _, _, u_s, c_s = _in_proj(xs.reshape(1, bs, D_MODEL), g_mix, wq_sb, wkt, wvt, wr)
        sbk_s.append(heads_last(kt_s).reshape(bs, 1, SB_HEADS, SB_HEAD_DIM))
        sbv_s.append(heads_last(vt_s).reshape(bs, 1, SB_HEADS, SB_HEAD_DIM))
        u_s, c_s = u_s[0], c_s[0]
        pool_s.append(jnp.concatenate([state_pool[l, :, 1:], u_s[:, None]], axis=1))
        conv_s.append(jnp.concatenate([state_conv[l, :, 1:], c_s[:, None]], axis=1))
        a, a_s = _sb_attention(l, bias2, q, ktb, vtb, tri_prompt, page_table, q_s.reshape(bs, 1, SB_WIDTH),
                               bias_rows, tri_page, pages_mat, cache_k, cache_v)

        bc_s = _pool_conv_sample(l, u_s, c_s, pool_rows, conv_rows, pwin, wp, ps, conv_w[l], bdw, lng, lnb, pw)
        x1_s, qm_s = _out_proj(xs.reshape(bs, D_MODEL), a_s.reshape(bs, SB_WIDTH), bc_s, wo, g_mem, wq)

        bc = _pool_conv_prompt(u, c, wp, ps, conv_w[l], bdw, lng, lnb, pw)
        x1, qm = _out_proj(xp.reshape(bp * seq, D_MODEL), a.reshape(bp * seq, SB_WIDTH),
                           bc.reshape(bp * seq, POOL_WIDTH + CONV_WIDTH), wo, g_mem, wq)
        o = _mem_attn_prompt(l, qm.reshape(bp, seq, D_MODEL), mkb_all, mvb_all)
        xp, o_s = _mo_ffn(x1, o.reshape(bp * seq, D_MODEL), wmo, g_ffn, wup, wdn, g_fin,
                          mem=(l, qm_s.reshape(bs, MEM_HEADS, MEM_HEAD_DIM), cache_mem_k, cache_mem_v))
        xs = _mo_ffn(x1_s, o_s.reshape(bs, D_MODEL), wmo, g_ffn, wup, wdn, g_fin)

    return (xp.reshape(bp, seq, D_MODEL), xs.reshape(bs, 1, D_MODEL),
            jnp.stack(sbk_p), jnp.stack(sbv_p), jnp.stack(sbk_s), jnp.stack(sbv_s),
            mk_all.reshape(DEPTH, bp, mem_len, MEM_HEADS, MEM_HEAD_DIM),
            mv_all.reshape(DEPTH, bp, mem_len, MEM_HEADS, MEM_HEAD_DIM),
            jnp.stack(pool_p), jnp.stack(pool_s), jnp.stack(conv_p), jnp.stack(conv_s))
```

```python
import functools

import jax
import jax.numpy as jnp
from jax import lax
from jax.experimental import pallas as pl
from jax.experimental.pallas import tpu as pltpu

F32 = jnp.float32
BF16 = jnp.bfloat16

D_MODEL = 1024
DEPTH = 2
PAGE_SIZE = 128
SB_HEAD_DIM = 64
SB_WIDTH = D_MODEL // 2
SB_HEADS = SB_WIDTH // SB_HEAD_DIM
POOL_WINDOWS = (2, 4, 8, 16)
POOL_WIDTH = D_MODEL // 4
POOL_GROUP = POOL_WIDTH // len(POOL_WINDOWS)
POOL_STATE = max(POOL_WINDOWS) - 1
CONV_WIDTH = D_MODEL // 4
CONV_K = 31
CONV_STATE = CONV_K - 1
MEM_HEADS = 4
MEM_HEAD_DIM = D_MODEL // MEM_HEADS
D_FF = 4 * D_MODEL
EPS = 1e-6

SB_SCALE = SB_HEAD_DIM ** -0.5
LOG2E = 1.4426950408889634
MEM_SCALE = MEM_HEAD_DIM ** -0.5

LANES = 128
SUBLANES = 8
SB_TILE = 256
ROW_TILE = 512
SEQ_TILE = 512
HALO = 32
CONV_CHUNK = 64
FF_CHUNK = 1024


def _const_spec(shape):
    nd = len(shape)
    return pl.BlockSpec(shape, lambda *_: (0,) * nd, pipeline_mode=pl.Buffered(1))


def _rms(x, g):
    return x * lax.rsqrt(jnp.mean(x * x, axis=-1, keepdims=True) + EPS) * g


def _softplus2(z):
    return jnp.maximum(z, 0.0) + jnp.log(1.0 + jnp.exp2(-jnp.abs(z))) * LOG2E


def _dot(a, b):
    return jnp.dot(a, b, preferred_element_type=F32)


def _dot_nt(a, b):
    return lax.dot_general(a, b, (((1,), (1,)), ((), ())), preferred_element_type=F32)


def _mem_kv_kernel(m_ref, wk_ref, wv_ref, k_ref, v_ref, kb_ref, vb_ref):
    m = m_ref[...].astype(BF16)
    k = _dot(m, wk_ref[...])
    v = _dot(m, wv_ref[...])
    k_ref[...] = k
    v_ref[...] = v
    kb_ref[...] = k.astype(BF16)
    vb_ref[...] = v.astype(BF16)


def _mem_kv(mem_rows, wk, wv):
    rows = mem_rows.shape[0]
    tm = min(ROW_TILE, rows)
    w_spec = pl.BlockSpec((None, D_MODEL, D_MODEL), lambda l, i: (l, 0, 0))
    o_spec = pl.BlockSpec((None, tm, D_MODEL), lambda l, i: (l, i, 0))
    f = jax.ShapeDtypeStruct((DEPTH, rows, D_MODEL), F32)
    b = jax.ShapeDtypeStruct((DEPTH, rows, D_MODEL), BF16)
    return pl.pallas_call(
        _mem_kv_kernel,
        grid=(DEPTH, rows // tm),
        in_specs=[pl.BlockSpec((tm, D_MODEL), lambda l, i: (i, 0)), w_spec, w_spec],
        out_specs=[o_spec, o_spec, o_spec, o_spec],
        out_shape=[f, f, b, b],
        name="mem_kv",
    )(mem_rows, wk, wv)


def _in_proj_kernel(x_ref, g_ref, wq_ref, wkt_ref, wvt_ref, wr_ref, q_ref, kt_ref, vt_ref, ktb_ref, vtb_ref,
                    u_ref, c_ref):
    h = _rms(x_ref[...], g_ref[...]).astype(BF16)
    q_ref[...] = (_dot(h, wq_ref[...]) * (SB_SCALE * LOG2E)).astype(BF16)
    kt = _dot_nt(wkt_ref[...], h)
    kt_ref[...] = kt
    ktb_ref[...] = kt.astype(BF16)
    vt = _dot_nt(wvt_ref[...], h)
    vt_ref[...] = vt
    vtb_ref[...] = vt.astype(BF16)
    u_ref[...] = _dot(h, wr_ref[:, 0:POOL_WIDTH])
    glu_a = _dot(h, wr_ref[:, POOL_WIDTH:POOL_WIDTH + CONV_WIDTH])
    glu_g = _dot(h, wr_ref[:, POOL_WIDTH + CONV_WIDTH:])
    c_ref[...] = glu_a * jax.nn.sigmoid(glu_g)


def _in_proj(x, g, wq, wkt, wvt, wr):
    b, seq, _ = x.shape
    tm = min(ROW_TILE, seq)

    def row_spec(width):
        return pl.BlockSpec((None, tm, width), lambda bi, i: (bi, i, 0))

    t_spec = pl.BlockSpec((None, SB_WIDTH, tm), lambda bi, i: (bi, 0, i))

    def rows(width, dtype):
        return jax.ShapeDtypeStruct((b, seq, width), dtype)

    def cols(dtype):
        return jax.ShapeDtypeStruct((b, SB_WIDTH, seq), dtype)

    return pl.pallas_call(
        _in_proj_kernel,
        grid=(b, seq // tm),
        in_specs=[row_spec(D_MODEL), _const_spec((1, D_MODEL)), _const_spec(wq.shape), _const_spec(wkt.shape),
                  _const_spec(wvt.shape), _const_spec(wr.shape)],
        out_specs=[row_spec(SB_WIDTH), t_spec, t_spec, t_spec, t_spec, row_spec(POOL_WIDTH),
                   row_spec(CONV_WIDTH)],
        out_shape=[rows(SB_WIDTH, BF16), cols(F32), cols(F32), cols(BF16), cols(BF16), rows(POOL_WIDTH, F32),
                   rows(CONV_WIDTH, F32)],
        name="in_proj",
    )(x, g, wq, wkt, wvt, wr)


def _sb_kernel(layer, n_pages, per_step, grid, pt_ref, bias_ref, q_ref, k_ref, v_ref, tri_ref, qs_ref, brow_ref,
               tpage_ref, pages_ref, ck_ref, cv_ref, o_ref, os_ref, acc_ref, run_ref, sp_ref, lsz_ref, ws_ref,
               page_buf, page_sem):
    n_items = 2 * per_step * n_pages
    n_steps = grid[0] * grid[1]
    step = pl.program_id(0) * grid[1] + pl.program_id(1)
    slot = lax.rem(step, 2)

    def page_copy(for_step, to_slot, n):
        r, kv, p = n // (2 * n_pages), (n // n_pages) % 2, n % n_pages
        cache = cv_ref if kv else ck_ref
        page = pt_ref[for_step * per_step + r, p]
        return pltpu.make_async_copy(cache.at[layer, page], page_buf.at[to_slot, n], page_sem.at[to_slot])

    @pl.when(step == 0)
    def _():
        for n in range(n_items):
            page_copy(0, 0, n).start()

    for n in range(n_items):
        page_copy(step, slot, n).wait()
    next_step = lax.rem(step + 1, n_steps)

    def prefetch(h):
        for n in range(h * n_items // SB_HEADS, (h + 1) * n_items // SB_HEADS):
            page_copy(next_step, 1 - slot, n).start()

    page_refs = [page_buf.at[slot, n] for n in range(n_items)]

    rows = n_pages * SB_HEADS
    head_of_lane = lax.broadcasted_iota(jnp.int32, (SB_HEADS, SB_WIDTH), 1) // SB_HEAD_DIM
    own = head_of_lane == lax.broadcasted_iota(jnp.int32, (SB_HEADS, SB_WIDTH), 0)
    qs_heads = [jnp.where(own, jnp.broadcast_to(qs_ref[r].astype(F32), (SB_HEADS, SB_WIDTH)), 0.0).astype(BF16)
                for r in range(per_step)]
    items = [(r, p) for r in range(per_step) for p in range(n_pages)]
    z_parts = [None] * len(items)
    accs = [jnp.zeros((SB_HEADS, SB_WIDTH), F32) for _ in range(per_step)]

    def item_slice(h):
        return range(h * len(items) // SB_HEADS, (h + 1) * len(items) // SB_HEADS)

    def sample_scores(h):
        prefetch(h)
        for n in item_slice(h):
            r, p = items[n]
            page = page_refs[2 * r * n_pages + p]
            z_parts[n] = _dot(qs_heads[r], page[...].astype(BF16)) + brow_ref[0:SB_HEADS, :]

    def sample_weights():
        z = jnp.concatenate(z_parts, axis=0)
        sp = _softplus2(z)
        sp_b = sp.astype(BF16)
        later_in_page = _dot(sp_b, tpage_ref[...])
        later_pages = jnp.concatenate(
            [jnp.sum(_dot(pages_ref[...], sp_b[r * rows:(r + 1) * rows]), axis=1, keepdims=True)
             for r in range(per_step)], axis=0)
        ws_ref[...] = jnp.exp2((z - sp) + later_in_page + later_pages)

    def sample_values(h):
        for n in item_slice(h):
            r, p = items[n]
            page = page_refs[(2 * r + 1) * n_pages + p]
            lo = r * rows + p * SB_HEADS
            accs[r] = accs[r] + _dot_nt(ws_ref[lo:lo + SB_HEADS, :].astype(BF16), page[...].astype(BF16))

    t = SB_TILE
    i = pl.program_id(1)
    first_head = lax.broadcasted_iota(jnp.int32, (t, LANES), 1) < SB_HEAD_DIM
    q_heads = []
    for p in range(SB_HEADS // 2):
        qp = q_ref[0, :, p * LANES:(p + 1) * LANES].astype(F32)
        q_heads += [jnp.where(first_head, qp, 0.0).astype(BF16), jnp.where(first_head, 0.0, qp).astype(BF16)]
    visible = (lax.broadcasted_iota(jnp.int32, (t, t), 1) < lax.broadcasted_iota(jnp.int32, (t, t), 0))
    acc_ref[...] = jnp.zeros_like(acc_ref)
    run_ref[...] = jnp.zeros_like(run_ref)

    def pair_rows(h):
        return slice((h // 2) * LANES, (h // 2 + 1) * LANES)

    def scores(j, masked, between=None):
        start = pl.multiple_of(j * t, t)
        for h in range(SB_HEADS):
            if between is not None:
                between(h)
            z = _dot(q_heads[h], k_ref[0, pair_rows(h), pl.ds(start, t)]) + bias_ref[h]
            sp = _softplus2(z)
            lsz = z - sp
            if masked:
                sp = jnp.where(visible, sp, 0.0)
                lsz = jnp.where(visible, lsz, -jnp.inf)
            sp_ref[h] = sp.astype(BF16)
            lsz_ref[h] = lsz

    def weights(j, between=None):
        start = pl.multiple_of(j * t, t)
        for h in range(SB_HEADS):
            if between is not None:
                between(h)
            sp = sp_ref[h]
            later = _dot(sp, tri_ref[...])
            run = run_ref[h]
            w = jnp.exp2(lsz_ref[h] + later + jnp.concatenate([run, run], axis=1))
            acc_ref[h] += _dot_nt(w.astype(BF16), v_ref[0, pair_rows(h), pl.ds(start, t)])
            run_ref[h] = run + (later[:, :1] - sp[:, :LANES].astype(F32)[:, :1])

    scores(i, True, between=sample_scores)
    sample_weights()

    def key_tile_pair(jj, carry):
        weights(i - jj)
        scores(i - 1 - jj, False)
        return carry

    lax.fori_loop(0, i, key_tile_pair, 0)
    weights(0, between=sample_values)
    for r in range(per_step):
        os_ref[r] = jnp.sum(jnp.where(own, accs[r], 0.0), axis=0, keepdims=True).astype(BF16)

    @pl.when(step == n_steps - 1)
    def _():
        for n in range(n_items):
            page_copy(next_step, 1 - slot, n).wait()

    for p in range(SB_HEADS // 2):
        o_ref[0, :, p * LANES:(p + 1) * LANES] = jnp.where(first_head, acc_ref[2 * p], acc_ref[2 * p + 1]).astype(BF16)


def _sb_attention(layer, bias2, q, kt, vt, tri, page_table, qs, bias_rows, tri_page, pages_mat, cache_k, cache_v):
    b, seq, _ = q.shape
    t = SB_TILE
    nq = seq // t
    bs, n_pages = page_table.shape
    assert bs % (b * nq) == 0
    per_step = bs // (b * nq)
    rows = n_pages * SB_HEADS
    kv_spec = pl.BlockSpec((1, SB_WIDTH, seq), lambda bi, i, pt: (bi, 0, 0))
    qo_spec = pl.BlockSpec((1, t, SB_WIDTH), lambda bi, i, pt: (bi, i, 0))
    sv_spec = pl.BlockSpec((per_step, 1, SB_WIDTH), lambda bi, i, pt: (bi * nq + i, 0, 0))

    hbm = pl.BlockSpec(memory_space=pl.ANY)
    grid_spec = pltpu.PrefetchScalarGridSpec(
        num_scalar_prefetch=1,
        grid=(b, nq),
        in_specs=[pl.BlockSpec(memory_space=pltpu.SMEM), qo_spec, kv_spec, kv_spec, _const_spec((t, t)), sv_spec,
                  _const_spec((rows, LANES)), _const_spec((PAGE_SIZE, PAGE_SIZE)), _const_spec((rows, rows)), hbm, hbm],
        out_specs=[qo_spec, sv_spec],
        scratch_shapes=[pltpu.VMEM((SB_HEADS, t, LANES), F32), pltpu.VMEM((SB_HEADS, t, LANES), F32),
                        pltpu.VMEM((SB_HEADS, t, t), BF16), pltpu.VMEM((SB_HEADS, t, t), F32),
                        pltpu.VMEM((per_step * rows, PAGE_SIZE), F32),
                        pltpu.VMEM((2, 2 * per_step * n_pages, SB_WIDTH, PAGE_SIZE), F32),
                        pltpu.SemaphoreType.DMA((2,))],
    )
    return pl.pallas_call(
        functools.partial(_sb_kernel, layer, n_pages, per_step, (b, nq)),
        grid_spec=grid_spec,
        out_shape=[jax.ShapeDtypeStruct((b, seq, SB_WIDTH), BF16), jax.ShapeDtypeStruct((bs, 1, SB_WIDTH), BF16)],
        compiler_params=pltpu.CompilerParams(dimension_semantics=("arbitrary", "arbitrary")),
        name="sb_attention",
    )(page_table, bias2, q, kt, vt, tri, qs, bias_rows, tri_page, pages_mat, cache_k, cache_v)


def _layer_norm_silu(y, g, b):
    mu = jnp.mean(y, axis=-1, keepdims=True)
    yc = y - mu
    var = jnp.mean(yc * yc, axis=-1, keepdims=True)
    y = yc * lax.rsqrt(var + EPS) * g + b
    return y * jax.nn.sigmoid(y)


def _pool_conv_prompt_kernel(u_ref, c_ref, wp_ref, ps_ref, wdw_ref, bdw_ref, lng_ref, lnb_ref, pw_ref,
                             o_ref, us, cs, ys, part, zs):
    tl = SEQ_TILE
    ti = pl.program_id(1)

    @pl.when(ti == 0)
    def _():
        us[0:HALO] = jnp.zeros((HALO, POOL_WIDTH), F32)
        cs[0:HALO] = jnp.zeros((HALO, CONV_WIDTH), F32)

    @pl.when(ti > 0)
    def _():
        us[0:HALO] = us[tl:tl + HALO]
        cs[0:HALO] = cs[tl:tl + HALO]

    u = u_ref[0]
    us[HALO:HALO + tl] = u
    cs[HALO:HALO + tl] = c_ref[0]

    end = HALO + tl
    part[0, 8:end] = us[8:end] + us[7:end - 1]
    part[1, 16:end] = part[0, 16:end] + part[0, 14:end - 2]
    part[2, 24:end] = part[1, 24:end] + part[1, 20:end - 4]
    s2 = part[0, HALO:end, 0:LANES]
    s4 = part[1, HALO:end, 0:LANES]
    s8 = part[2, HALO:end, LANES:]
    s16 = s8 + part[2, HALO - 8:end - 8, LANES:]
    low_group = lax.broadcasted_iota(jnp.int32, (tl, LANES), 1) < POOL_GROUP
    seen = ti * tl + lax.broadcasted_iota(jnp.int32, (tl, LANES), 0) + 1
    cnt_lo = jnp.minimum(jnp.where(low_group, POOL_WINDOWS[0], POOL_WINDOWS[1]), seen).astype(F32)
    cnt_hi = jnp.minimum(jnp.where(low_group, POOL_WINDOWS[2], POOL_WINDOWS[3]), seen).astype(F32)
    mean = jnp.concatenate([jnp.where(low_group, s2, s4) / cnt_lo,
                            jnp.where(low_group, s8, s16) / cnt_hi], axis=1)
    pool = _dot((mean - u).astype(BF16), wp_ref[...]) * ps_ref[...]

    lead = HALO - CONV_STATE
    for r0 in range(0, tl, CONV_CHUNK):
        acc = jnp.broadcast_to(bdw_ref[...], (CONV_CHUNK, CONV_WIDTH))
        for phase in range(SUBLANES):
            rows = CONV_CHUNK if phase == 0 else CONV_CHUNK + SUBLANES
            z = None
            for k in range(CONV_K):
                if (k + lead) % SUBLANES == phase:
                    term = cs[pl.ds(r0 + k + lead - phase, rows), :] * wdw_ref[k:k + 1, :]
                    z = term if z is None else z + term
            if phase == 0:
                acc = acc + z
            else:
                zs[phase - 1] = z
                acc = acc + zs[phase - 1, pl.ds(phase, CONV_CHUNK), :]
        ys[r0:r0 + CONV_CHUNK] = acc
    conv = _dot(_layer_norm_silu(ys[...], lng_ref[...], lnb_ref[...]).astype(BF16), pw_ref[...])
    o_ref[0] = jnp.concatenate([pool, conv], axis=1).astype(BF16)


def _pool_conv_prompt(u, c, wp, ps, wdw, bdw, lng, lnb, pw):
    b, seq, _ = u.shape
    tl = SEQ_TILE
    vec = _const_spec((1, CONV_WIDTH))
    return pl.pallas_call(
        _pool_conv_prompt_kernel,
        grid=(b, seq // tl),
        in_specs=[pl.BlockSpec((1, tl, POOL_WIDTH), lambda bi, i: (bi, i, 0)),
                  pl.BlockSpec((1, tl, CONV_WIDTH), lambda bi, i: (bi, i, 0)),
                  _const_spec(wp.shape), vec, _const_spec(wdw.shape), vec, vec, vec, _const_spec(pw.shape)],
        out_specs=pl.BlockSpec((1, tl, POOL_WIDTH + CONV_WIDTH), lambda bi, i: (bi, i, 0)),
        out_shape=jax.ShapeDtypeStruct((b, seq, POOL_WIDTH + CONV_WIDTH), BF16),
        scratch_shapes=[pltpu.VMEM((HALO + tl, POOL_WIDTH), F32), pltpu.VMEM((HALO + tl, CONV_WIDTH), F32),
                        pltpu.VMEM((tl, CONV_WIDTH), F32), pltpu.VMEM((3, HALO + tl, POOL_WIDTH), F32),
                        pltpu.VMEM((SUBLANES - 1, CONV_CHUNK + SUBLANES, CONV_WIDTH), F32)],
        compiler_params=pltpu.CompilerParams(dimension_semantics=("arbitrary", "arbitrary")),
        name="pool_conv_prompt",
    )(u, c, wp, ps, wdw, bdw, lng, lnb, pw)


def _pool_conv_sample_kernel(u_ref, c_ref, pprev_ref, cprev_ref, pwin_ref, wp_ref, ps_ref, wdw_ref, bdw_ref,
                             lng_ref, lnb_ref, pw_ref, o_ref):
    u = u_ref[...]
    c = c_ref[...]
    mean = u * pwin_ref[POOL_STATE:POOL_STATE + 1, :]
    for k in range(POOL_STATE):
        mean = mean + pprev_ref[k] * pwin_ref[k:k + 1, :]
    pool = _dot((mean - u).astype(BF16), wp_ref[...]) * ps_ref[...]
    y = c * wdw_ref[CONV_STATE:CONV_K, :] + bdw_ref[...]
    for k in range(CONV_STATE):
        y = y + cprev_ref[k] * wdw_ref[k:k + 1, :]
    conv = _dot(_layer_norm_silu(y, lng_ref[...], lnb_ref[...]).astype(BF16), pw_ref[...])
    o_ref[...] = jnp.concatenate([pool, conv], axis=1).astype(BF16)


def _pool_conv_sample(layer, u, c, pool_rows, conv_rows, pwin, wp, ps, wdw, bdw, lng, lnb, pw):
    bs = u.shape[0]
    tb = 32
    vec = _const_spec((1, CONV_WIDTH))
    return pl.pallas_call(
        _pool_conv_sample_kernel,
        grid=(bs // tb,),
        in_specs=[pl.BlockSpec((tb, POOL_WIDTH), lambda i: (i, 0)),
                  pl.BlockSpec((tb, CONV_WIDTH), lambda i: (i, 0)),
                  pl.BlockSpec((None, POOL_STATE, tb, POOL_WIDTH), lambda i: (layer, 0, i, 0)),
                  pl.BlockSpec((None, CONV_STATE, tb, CONV_WIDTH), lambda i: (layer, 0, i, 0)),
                  _const_spec(pwin.shape), _const_spec(wp.shape), vec, _const_spec(wdw.shape), vec, vec, vec,
                  _const_spec(pw.shape)],
        out_specs=pl.BlockSpec((tb, POOL_WIDTH + CONV_WIDTH), lambda i: (i, 0)),
        out_shape=jax.ShapeDtypeStruct((bs, POOL_WIDTH + CONV_WIDTH), BF16),
        name="pool_conv_sample",
    )(u, c, pool_rows, conv_rows, pwin, wp, ps, wdw, bdw, lng, lnb, pw)


def _out_proj_kernel(x_ref, a_ref, bc_ref, wo_ref, g_ref, wq_ref, x1_ref, q_ref):
    x1 = x_ref[...] + _dot(a_ref[...], wo_ref[0:SB_WIDTH, :]) + _dot(bc_ref[...], wo_ref[SB_WIDTH:, :])
    x1_ref[...] = x1
    h = _rms(x1, g_ref[...]).astype(BF16)
    q_ref[...] = (_dot(h, wq_ref[...]) * MEM_SCALE).astype(BF16)


def _out_proj(x, a, bc, wo, g, wq):
    rows = x.shape[0]
    tm = min(ROW_TILE, rows)

    def row_spec(width):
        return pl.BlockSpec((tm, width), lambda i: (i, 0))

    return pl.pallas_call(
        _out_proj_kernel,
        grid=(rows // tm,),
        in_specs=[row_spec(D_MODEL), row_spec(SB_WIDTH), row_spec(POOL_WIDTH + CONV_WIDTH),
                  _const_spec(wo.shape), _const_spec((1, D_MODEL)), _const_spec(wq.shape)],
        out_specs=[row_spec(D_MODEL), row_spec(D_MODEL)],
        out_shape=[jax.ShapeDtypeStruct((rows, D_MODEL), F32), jax.ShapeDtypeStruct((rows, D_MODEL), BF16)],
        name="out_proj",
    )(x, a, bc, wo, g, wq)


def _mem_attn_prompt_kernel(q_ref, mk_ref, mv_ref, o_ref):
    outs = []
    for h in range(MEM_HEADS):
        hs = slice(h * MEM_HEAD_DIM, (h + 1) * MEM_HEAD_DIM)
        s = _dot_nt(q_ref[0, :, hs], mk_ref[:, hs])
        e = jnp.exp(s - jnp.max(s, axis=-1, keepdims=True))
        den = jnp.sum(e, axis=-1, keepdims=True)
        outs.append(_dot((e / den).astype(BF16), mv_ref[:, hs]))
    o_ref[0] = jnp.concatenate(outs, axis=1).astype(BF16)


def _mem_attn_prompt(layer, q, mk, mv):
    b, seq, _ = q.shape
    tm = ROW_TILE
    mem_len = mk.shape[2]
    m_spec = pl.BlockSpec((None, None, mem_len, D_MODEL), lambda bi, i: (layer, bi, 0, 0))
    qo_spec = pl.BlockSpec((1, tm, D_MODEL), lambda bi, i: (bi, i, 0))
    return pl.pallas_call(
        _mem_attn_prompt_kernel,
        grid=(b, seq // tm),
        in_specs=[qo_spec, m_spec, m_spec],
        out_specs=qo_spec,
        out_shape=jax.ShapeDtypeStruct((b, seq, D_MODEL), BF16),
        name="mem_attn_prompt",
    )(q, mk, mv)


def _mem_attn_one(q, mk_ref, mv_ref):
    q = q.astype(F32)
    s = jnp.sum(mk_ref[...] * q[None], axis=-1, keepdims=True)
    e = jnp.exp(s - jnp.max(s, axis=0, keepdims=True))
    p = e / jnp.sum(e, axis=0, keepdims=True)
    return jnp.sum(p * mv_ref[...], axis=0)


def _mo_ffn_kernel(final, mem_per_step, x1_ref, o_ref, wmo_ref, g_ref, wup_ref, wdn_ref, *refs):
    refs = list(refs)
    gf_ref = refs.pop(0) if final else None
    if mem_per_step:
        qs_ref, mk_ref, mv_ref = refs[:3]
        out_ref, os_ref = refs[3:]
        for r in range(mem_per_step):
            os_ref[r] = _mem_attn_one(qs_ref[r], mk_ref.at[r], mv_ref.at[r]).astype(BF16)
    else:
        (out_ref,) = refs
    x2 = x1_ref[...] + _dot(o_ref[...], wmo_ref[...])
    h = _rms(x2, g_ref[...]).astype(BF16)
    acc = x2
    for c0 in range(0, D_FF, FF_CHUNK):
        up = jnp.maximum(_dot(h, wup_ref[:, c0:c0 + FF_CHUNK]), 0.0)
        acc = acc + _dot((up * up).astype(BF16), wdn_ref[c0:c0 + FF_CHUNK, :])
    if final:
        acc = _rms(acc, gf_ref[...])
    out_ref[...] = acc


def _mo_ffn(x1, o, wmo, g, wup, wdn, g_final=None, mem=None):
    rows = x1.shape[0]
    tm = min(ROW_TILE, rows)
    steps = rows // tm
    row_spec = pl.BlockSpec((tm, D_MODEL), lambda i: (i, 0))
    final = g_final is not None
    in_specs = [row_spec, row_spec, _const_spec(wmo.shape), _const_spec((1, D_MODEL)), _const_spec(wup.shape),
                _const_spec(wdn.shape)]
    args = [x1, o, wmo, g, wup, wdn]
    out_specs = [row_spec]
    out_shape = [jax.ShapeDtypeStruct((rows, D_MODEL), F32)]
    if final:
        in_specs.append(_const_spec((1, D_MODEL)))
        args.append(g_final)
    per_step = 0
    if mem is not None:
        layer, qs, cache_k, cache_v = mem
        bs, mem_len = qs.shape[0], cache_k.shape[2]
        assert bs % steps == 0
        per_step = bs // steps
        vec_spec = pl.BlockSpec((per_step, MEM_HEADS, MEM_HEAD_DIM), lambda i: (i, 0, 0))
        m_spec = pl.BlockSpec((None, per_step, mem_len, MEM_HEADS, MEM_HEAD_DIM), lambda i: (layer, i, 0, 0, 0))
        in_specs += [vec_spec, m_spec, m_spec]
        args += [qs, cache_k, cache_v]
        out_specs.append(vec_spec)
        out_shape.append(jax.ShapeDtypeStruct((bs, MEM_HEADS, MEM_HEAD_DIM), BF16))
    outs = pl.pallas_call(
        functools.partial(_mo_ffn_kernel, final, per_step),
        grid=(steps,),
        in_specs=in_specs,
        out_specs=out_specs,
        out_shape=out_shape,
        name="mo_ffn",
    )(*args)
    return outs if mem is not None else outs[0]


def _suffix_triangle(n):
    j = lax.broadcasted_iota(jnp.int32, (n, n), 0)
    s = lax.broadcasted_iota(jnp.int32, (n, n), 1)
    return jnp.where(j > s, -1.0, 0.0).astype(BF16)


def _later_pages_matrix(n_pages):
    n = n_pages * SB_HEADS
    r = lax.broadcasted_iota(jnp.int32, (n, n), 0)
    c = lax.broadcasted_iota(jnp.int32, (n, n), 1)
    same_head = (r % SB_HEADS) == (c % SB_HEADS)
    return jnp.where(same_head & (c // SB_HEADS > r // SB_HEADS), -1.0, 0.0).astype(BF16)


def _pool_window_rows():
    row = lax.broadcasted_iota(jnp.int32, (POOL_STATE + 1, POOL_WIDTH), 0)
    ch = lax.broadcasted_iota(jnp.int32, (POOL_STATE + 1, POOL_WIDTH), 1)
    w = jnp.zeros((POOL_STATE + 1, POOL_WIDTH), jnp.int32)
    for g, win in enumerate(POOL_WINDOWS):
        w = jnp.where(ch // POOL_GROUP == g, win, w)
    return jnp.where(row >= POOL_STATE + 1 - w, 1.0 / w.astype(F32), 0.0)


def _block_diag(w_grp):
    n, g, _ = w_grp.shape
    out = jnp.zeros((n * g, n * g), w_grp.dtype)
    for i in range(n):
        out = lax.dynamic_update_slice(out, w_grp[i], (i * g, i * g))
    return out


def kernel(x_prompt, x_sample, cache_sb_k, cache_sb_v, cache_mem_k, cache_mem_v, state_pool, state_conv,
           page_table, mem_prompt, norm_mix_g, w_in, sb_bias, pool_w, pool_scale, conv_w, conv_b, conv_ln_g,
           conv_ln_b, conv_pw, w_out, norm_mem_g, w_mq, w_mk, w_mv, w_mo, norm_ffn_g, w_up, w_down,
           norm_final_g):
    bp, seq, _ = x_prompt.shape
    bs = x_sample.shape[0]
    n_pages = page_table.shape[1]
    n_phys = cache_sb_k.shape[1]
    mem_len = mem_prompt.shape[1]
    assert x_sample.shape[1] == 1 and seq % SB_TILE == 0 and seq >= CONV_STATE

    cache_k = cache_sb_k.transpose(0, 1, 3, 4, 2).reshape(DEPTH, n_phys, SB_WIDTH, PAGE_SIZE)
    cache_v = cache_sb_v.transpose(0, 1, 3, 4, 2).reshape(DEPTH, n_phys, SB_WIDTH, PAGE_SIZE)
    pool_rows = state_pool.transpose(0, 2, 1, 3)
    conv_rows = state_conv.transpose(0, 2, 1, 3)

    tri_prompt = _suffix_triangle(SB_TILE)
    tri_page = _suffix_triangle(PAGE_SIZE)
    pages_mat = _later_pages_matrix(n_pages)
    pwin = _pool_window_rows()

    mk_all, mv_all, mkb_all, mvb_all = _mem_kv(mem_prompt.reshape(bp * mem_len, D_MODEL),
                                                w_mk.astype(BF16), w_mv.astype(BF16))
    mkb_all = mkb_all.reshape(DEPTH, bp, mem_len, D_MODEL)
    mvb_all = mvb_all.reshape(DEPTH, bp, mem_len, D_MODEL)

    def heads_last(t):
        return t.reshape(t.shape[0], SB_HEADS, SB_HEAD_DIM, t.shape[2]).transpose(0, 3, 1, 2)

    xp = x_prompt
    xs = x_sample
    sbk_p, sbv_p, sbk_s, sbv_s = [], [], [], []
    pool_p, pool_s, conv_p, conv_s = [], [], [], []
    for l in range(DEPTH):
        last = l == DEPTH - 1
        g_mix = norm_mix_g[l].reshape(1, D_MODEL)
        g_mem = norm_mem_g[l].reshape(1, D_MODEL)
        g_ffn = norm_ffn_g[l].reshape(1, D_MODEL)
        g_fin = norm_final_g.reshape(1, D_MODEL) if last else None
        wq_sb = w_in[l, :, 0:SB_WIDTH].astype(BF16)
        wkt = w_in[l, :, SB_WIDTH:2 * SB_WIDTH].T.astype(BF16)
        wvt = w_in[l, :, 2 * SB_WIDTH:3 * SB_WIDTH].T.astype(BF16)
        wr = w_in[l, :, 3 * SB_WIDTH:].astype(BF16)
        wo, wq, wmo = (w.astype(BF16) for w in (w_out[l], w_mq[l], w_mo[l]))
        wup, wdn = w_up[l].astype(BF16), w_down[l].astype(BF16)
        wp = _block_diag(pool_w[l]).astype(BF16)
        ps = pool_scale[l].reshape(1, POOL_WIDTH)
        bdw = conv_b[l].reshape(1, CONV_WIDTH)
        lng = conv_ln_g[l].reshape(1, CONV_WIDTH)
        lnb = conv_ln_b[l].reshape(1, CONV_WIDTH)
        pw = conv_pw[l].astype(BF16)
        bias2 = sb_bias[l] * LOG2E
        bias_rows = jnp.broadcast_to(jnp.tile(bias2, n_pages)[:, None], (n_pages * SB_HEADS, LANES))

        q, kt, vt, ktb, vtb, u, c = _in_proj(xp.reshape(bp, seq, D_MODEL), g_mix, wq_sb, wkt, wvt, wr)
        sbk_p.append(heads_last(kt))
        sbv_p.append(heads_last(vt))
        pool_p.append(u[:, seq - POOL_STATE:])
        conv_p.append(c[:, seq - CONV_STATE:])
        q_s, kt_s, vt_s, _, _, u_s, c_s = _in_proj(xs.reshape(1, bs, D_MODEL), g_mix, wq_sb, wkt, wvt, wr)
        sbk_s.append(heads_last(kt_s).reshape(bs, 1, SB_HEADS, SB_HEAD_DIM))
        sbv_s.append(heads_last(vt_s).reshape(bs, 1, SB_HEADS, SB_HEAD_DIM))
        u_s, c_s = u_s[0], c_s[0]
        pool_s.append(jnp.concatenate([state_pool[l, :, 1:], u_s[:, None]], axis=1))
        conv_s.append(jnp.concatenate([state_conv[l, :, 1:], c_s[:, None]], axis=1))
        a, a_s = _sb_attention(l, bias2, q, ktb, vtb, tri_prompt, page_table, q_s.reshape(bs, 1, SB_WIDTH),
                               bias_rows, tri_page, pages_mat, cache_k, cache_v)

        bc_s = _pool_conv_sample(l, u_s, c_s, pool_rows, conv_rows, pwin, wp, ps, conv_w[l], bdw, lng, lnb, pw)
        x1_s, qm_s = _out_proj(xs.reshape(bs, D_MODEL), a_s.reshape(bs, SB_WIDTH), bc_s, wo, g_mem, wq)

        bc = _pool_conv_prompt(u, c, wp, ps, conv_w[l], bdw, lng, lnb, pw)
        x1, qm = _out_proj(xp.reshape(bp * seq, D_MODEL), a.reshape(bp * seq, SB_WIDTH),
                           bc.reshape(bp * seq, POOL_WIDTH + CONV_WIDTH), wo, g_mem, wq)
        o = _mem_attn_prompt(l, qm.reshape(bp, seq, D_MODEL), mkb_all, mvb_all)
        xp, o_s = _mo_ffn(x1, o.reshape(bp * seq, D_MODEL), wmo, g_ffn, wup, wdn, g_fin,
                          mem=(l, qm_s.reshape(bs, MEM_HEADS, MEM_HEAD_DIM), cache_mem_k, cache_mem_v))
        xs = _mo_ffn(x1_s, o_s.reshape(bs, D_MODEL), wmo, g_ffn, wup, wdn, g_fin)

    return (xp.reshape(bp, seq, D_MODEL), xs.reshape(bs, 1, D_MODEL),
            jnp.stack(sbk_p), jnp.stack(sbv_p), jnp.stack(sbk_s), jnp.stack(sbv_s),
            mk_all.reshape(DEPTH, bp, mem_len, MEM_HEADS, MEM_HEAD_DIM),
            mv_all.reshape(DEPTH, bp, mem_len, MEM_HEADS, MEM_HEAD_DIM),
            jnp.stack(pool_p), jnp.stack(pool_s), jnp.stack(conv_p), jnp.stack(conv_s))
```

```python
import functools

import jax
import jax.numpy as jnp
from jax import lax
from jax.experimental import pallas as pl
from jax.experimental.pallas import tpu as pltpu

F32 = jnp.float32
BF16 = jnp.bfloat16

D_MODEL = 1024
DEPTH = 2
PAGE_SIZE = 128
SB_HEAD_DIM = 64
SB_WIDTH = D_MODEL // 2
SB_HEADS = SB_WIDTH // SB_HEAD_DIM
POOL_WINDOWS = (2, 4, 8, 16)
POOL_WIDTH = D_MODEL // 4
POOL_GROUP = POOL_WIDTH // len(POOL_WINDOWS)
POOL_STATE = max(POOL_WINDOWS) - 1
CONV_WIDTH = D_MODEL // 4
CONV_K = 31
CONV_STATE = CONV_K - 1
MEM_HEADS = 4
MEM_HEAD_DIM = D_MODEL // MEM_HEADS
D_FF = 4 * D_MODEL
EPS = 1e-6

SB_SCALE = SB_HEAD_DIM ** -0.5
LOG2E = 1.4426950408889634
MEM_SCALE = MEM_HEAD_DIM ** -0.5

LANES = 128
SUBLANES = 8
SB_TILE = 256
ROW_TILE = 512
SEQ_TILE = 512
HALO = 32
CONV_CHUNK = 64
FF_CHUNK = 1024


def _const_spec(shape):
    nd = len(shape)
    return pl.BlockSpec(shape, lambda *_: (0,) * nd, pipeline_mode=pl.Buffered(1))


def _rms(x, g):
    return x * lax.rsqrt(jnp.mean(x * x, axis=-1, keepdims=True) + EPS) * g


def _softplus2(z):
    return jnp.maximum(z, 0.0) + jnp.log(1.0 + jnp.exp2(-jnp.abs(z))) * LOG2E


def _dot(a, b):
    return jnp.dot(a, b, preferred_element_type=F32)


def _dot_nt(a, b):
    return lax.dot_general(a, b, (((1,), (1,)), ((), ())), preferred_element_type=F32)


def _mem_kv_kernel(m_ref, wk_ref, wv_ref, k_ref, v_ref, kb_ref, vb_ref):
    m = m_ref[...].astype(BF16)
    k = _dot(m, wk_ref[...])
    v = _dot(m, wv_ref[...])
    k_ref[...] = k
    v_ref[...] = v
    kb_ref[...] = k.astype(BF16)
    vb_ref[...] = v.astype(BF16)


def _mem_kv(mem_rows, wk, wv):
    rows = mem_rows.shape[0]
    tm = min(ROW_TILE, rows)
    w_spec = pl.BlockSpec((None, D_MODEL, D_MODEL), lambda l, i: (l, 0, 0))
    o_spec = pl.BlockSpec((None, tm, D_MODEL), lambda l, i: (l, i, 0))
    f = jax.ShapeDtypeStruct((DEPTH, rows, D_MODEL), F32)
    b = jax.ShapeDtypeStruct((DEPTH, rows, D_MODEL), BF16)
    return pl.pallas_call(
        _mem_kv_kernel,
        grid=(DEPTH, rows // tm),
        in_specs=[pl.BlockSpec((tm, D_MODEL), lambda l, i: (i, 0)), w_spec, w_spec],
        out_specs=[o_spec, o_spec, o_spec, o_spec],
        out_shape=[f, f, b, b],
        name="mem_kv",
    )(mem_rows, wk, wv)


def _in_proj_kernel(x_ref, g_ref, wq_ref, wkt_ref, wvt_ref, wr_ref, q_ref, kt_ref, vt_ref, ktb_ref, vtb_ref,
                    u_ref, c_ref):
    h = _rms(x_ref[...], g_ref[...]).astype(BF16)
    q_ref[...] = (_dot(h, wq_ref[...]) * (SB_SCALE * LOG2E)).astype(BF16)
    kt = _dot_nt(wkt_ref[...], h)
    kt_ref[...] = kt
    ktb_ref[...] = kt.astype(BF16)
    vt = _dot_nt(wvt_ref[...], h)
    vt_ref[...] = vt
    vtb_ref[...] = vt.astype(BF16)
    u_ref[...] = _dot(h, wr_ref[:, 0:POOL_WIDTH])
    glu_a = _dot(h, wr_ref[:, POOL_WIDTH:POOL_WIDTH + CONV_WIDTH])
    glu_g = _dot(h, wr_ref[:, POOL_WIDTH + CONV_WIDTH:])
    c_ref[...] = glu_a * jax.nn.sigmoid(glu_g)


def _in_proj(x, g, wq, wkt, wvt, wr):
    b, seq, _ = x.shape
    tm = min(ROW_TILE, seq)

    def row_spec(width):
        return pl.BlockSpec((None, tm, width), lambda bi, i: (bi, i, 0))

    t_spec = pl.BlockSpec((None, SB_WIDTH, tm), lambda bi, i: (bi, 0, i))

    def rows(width, dtype):
        return jax.ShapeDtypeStruct((b, seq, width), dtype)

    def cols(dtype):
        return jax.ShapeDtypeStruct((b, SB_WIDTH, seq), dtype)

    return pl.pallas_call(
        _in_proj_kernel,
        grid=(b, seq // tm),
        in_specs=[row_spec(D_MODEL), _const_spec((1, D_MODEL)), _const_spec(wq.shape), _const_spec(wkt.shape),
                  _const_spec(wvt.shape), _const_spec(wr.shape)],
        out_specs=[row_spec(SB_WIDTH), t_spec, t_spec, t_spec, t_spec, row_spec(POOL_WIDTH),
                   row_spec(CONV_WIDTH)],
        out_shape=[rows(SB_WIDTH, BF16), cols(F32), cols(F32), cols(BF16), cols(BF16), rows(POOL_WIDTH, F32),
                   rows(CONV_WIDTH, F32)],
        name="in_proj",
    )(x, g, wq, wkt, wvt, wr)


def _sb_kernel(layer, n_pages, per_step, grid, pt_ref, bias_ref, q_ref, k_ref, v_ref, tri_ref, qs_ref, brow_ref,
               tpage_ref, pages_ref, ck_ref, cv_ref, o_ref, os_ref, acc_ref, run_ref, sp_ref, lsz_ref, ws_ref,
               page_buf, page_sem):
    n_items = 2 * per_step * n_pages
    n_steps = grid[0] * grid[1]
    step = pl.program_id(0) * grid[1] + pl.program_id(1)
    slot = lax.rem(step, 2)

    def page_copy(for_step, to_slot, n):
        r, kv, p = n // (2 * n_pages), (n // n_pages) % 2, n % n_pages
        cache = cv_ref if kv else ck_ref
        page = pt_ref[for_step * per_step + r, p]
        return pltpu.make_async_copy(cache.at[layer, page], page_buf.at[to_slot, n], page_sem.at[to_slot])

    @pl.when(step == 0)
    def _():
        for n in range(n_items):
            page_copy(0, 0, n).start()

    for n in range(n_items):
        page_copy(step, slot, n).wait()
    next_step = lax.rem(step + 1, n_steps)

    def prefetch(h):
        for n in range(h * n_items // SB_HEADS, (h + 1) * n_items // SB_HEADS):
            page_copy(next_step, 1 - slot, n).start()

    page_refs = [page_buf.at[slot, n] for n in range(n_items)]

    rows = n_pages * SB_HEADS
    head_of_lane = lax.broadcasted_iota(jnp.int32, (SB_HEADS, SB_WIDTH), 1) // SB_HEAD_DIM
    own = head_of_lane == lax.broadcasted_iota(jnp.int32, (SB_HEADS, SB_WIDTH), 0)
    qs_heads = [jnp.where(own, jnp.broadcast_to(qs_ref[r].astype(F32), (SB_HEADS, SB_WIDTH)), 0.0).astype(BF16)
                for r in range(per_step)]
    items = [(r, p) for r in range(per_step) for p in range(n_pages)]
    z_parts = [None] * len(items)

    def item_slice(h):
        return range(h * len(items) // SB_HEADS, (h + 1) * len(items) // SB_HEADS)

    def sample_scores(h):
        prefetch(h)
        for n in item_slice(h):
            r, p = items[n]
            page = page_refs[2 * r * n_pages + p]
            z_parts[n] = _dot(qs_heads[r], page[...].astype(BF16)) + brow_ref[0:SB_HEADS, :]

    def sample_weights():
        z = jnp.concatenate(z_parts, axis=0)
        sp = _softplus2(z)
        sp_b = sp.astype(BF16)
        later_in_page = _dot(sp_b, tpage_ref[...])
        later_pages = jnp.concatenate(
            [jnp.sum(_dot(pages_ref[...], sp_b[r * rows:(r + 1) * rows]), axis=1, keepdims=True)
             for r in range(per_step)], axis=0)
        ws_ref[...] = jnp.exp2((z - sp) + later_in_page + later_pages)

    def sample_values(h):
        for r in range(per_step):
            acc = jnp.zeros((SB_HEAD_DIM, PAGE_SIZE), F32)
            for p in range(n_pages):
                page = page_refs[(2 * r + 1) * n_pages + p]
                w_row = ws_ref[pl.ds(r * rows + p * SB_HEADS + h, 1), :]
                acc = acc + page[h * SB_HEAD_DIM:(h + 1) * SB_HEAD_DIM, :] * w_row
            os_ref[r, h * SB_HEAD_DIM:(h + 1) * SB_HEAD_DIM, :] = jnp.sum(acc, axis=1, keepdims=True)

    t = SB_TILE
    i = pl.program_id(1)
    first_head = lax.broadcasted_iota(jnp.int32, (t, LANES), 1) < SB_HEAD_DIM
    q_heads = []
    for p in range(SB_HEADS // 2):
        qp = q_ref[0, :, p * LANES:(p + 1) * LANES].astype(F32)
        q_heads += [jnp.where(first_head, qp, 0.0).astype(BF16), jnp.where(first_head, 0.0, qp).astype(BF16)]
    visible = (lax.broadcasted_iota(jnp.int32, (t, t), 1) < lax.broadcasted_iota(jnp.int32, (t, t), 0))
    acc_ref[...] = jnp.zeros_like(acc_ref)
    run_ref[...] = jnp.zeros_like(run_ref)

    def pair_rows(h):
        return slice((h // 2) * LANES, (h // 2 + 1) * LANES)

    def scores(j, masked, between=None):
        start = pl.multiple_of(j * t, t)
        for h in range(SB_HEADS):
            if between is not None:
                between(h)
            z = _dot(q_heads[h], k_ref[0, pair_rows(h), pl.ds(start, t)]) + bias_ref[h]
            sp = _softplus2(z)
            lsz = z - sp
            if masked:
                sp = jnp.where(visible, sp, 0.0)
                lsz = jnp.where(visible, lsz, -jnp.inf)
            sp_ref[h] = sp.astype(BF16)
            lsz_ref[h] = lsz

    def weights(j, between=None):
        start = pl.multiple_of(j * t, t)
        for h in range(SB_HEADS):
            if between is not None:
                between(h)
            sp = sp_ref[h]
            later = _dot(sp, tri_ref[...])
            run = run_ref[h]
            w = jnp.exp2(lsz_ref[h] + later + jnp.concatenate([run, run], axis=1))
            acc_ref[h] += _dot_nt(w.astype(BF16), v_ref[0, pair_rows(h), pl.ds(start, t)])
            run_ref[h] = run + (later[:, :1] - sp[:, :LANES].astype(F32)[:, :1])

    scores(i, True, between=sample_scores)
    sample_weights()

    def key_tile_pair(jj, carry):
        weights(i - jj)
        scores(i - 1 - jj, False)
        return carry

    lax.fori_loop(0, i, key_tile_pair, 0)
    weights(0, between=sample_values)

    @pl.when(step == n_steps - 1)
    def _():
        for n in range(n_items):
            page_copy(next_step, 1 - slot, n).wait()

    for p in range(SB_HEADS // 2):
        o_ref[0, :, p * LANES:(p + 1) * LANES] = jnp.where(first_head, acc_ref[2 * p], acc_ref[2 * p + 1]).astype(BF16)


def _sb_attention(layer, bias2, q, kt, vt, tri, page_table, qs, bias_rows, tri_page, pages_mat, cache_k, cache_v):
    b, seq, _ = q.shape
    t = SB_TILE
    nq = seq // t
    bs, n_pages = page_table.shape
    assert bs % (b * nq) == 0
    per_step = bs // (b * nq)
    rows = n_pages * SB_HEADS
    kv_spec = pl.BlockSpec((1, SB_WIDTH, seq), lambda bi, i, pt: (bi, 0, 0))
    qo_spec = pl.BlockSpec((1, t, SB_WIDTH), lambda bi, i, pt: (bi, i, 0))
    sv_spec = pl.BlockSpec((per_step, 1, SB_WIDTH), lambda bi, i, pt: (bi * nq + i, 0, 0))

    hbm = pl.BlockSpec(memory_space=pl.ANY)
    grid_spec = pltpu.PrefetchScalarGridSpec(
        num_scalar_prefetch=1,
        grid=(b, nq),
        in_specs=[pl.BlockSpec(memory_space=pltpu.SMEM), qo_spec, kv_spec, kv_spec, _const_spec((t, t)), sv_spec,
                  _const_spec((rows, LANES)), _const_spec((PAGE_SIZE, PAGE_SIZE)), _const_spec((rows, rows)), hbm, hbm],
        out_specs=[qo_spec, pl.BlockSpec((per_step, SB_WIDTH, 1), lambda bi, i, pt: (bi * nq + i, 0, 0))],
        scratch_shapes=[pltpu.VMEM((SB_HEADS, t, LANES), F32), pltpu.VMEM((SB_HEADS, t, LANES), F32),
                        pltpu.VMEM((SB_HEADS, t, t), BF16), pltpu.VMEM((SB_HEADS, t, t), F32),
                        pltpu.VMEM((per_step * rows, PAGE_SIZE), F32),
                        pltpu.VMEM((2, 2 * per_step * n_pages, SB_WIDTH, PAGE_SIZE), F32),
                        pltpu.SemaphoreType.DMA((2,))],
    )
    return pl.pallas_call(
        functools.partial(_sb_kernel, layer, n_pages, per_step, (b, nq)),
        grid_spec=grid_spec,
        out_shape=[jax.ShapeDtypeStruct((b, seq, SB_WIDTH), BF16), jax.ShapeDtypeStruct((bs, SB_WIDTH, 1), F32)],
        compiler_params=pltpu.CompilerParams(dimension_semantics=("arbitrary", "arbitrary")),
        name="sb_attention",
    )(page_table, bias2, q, kt, vt, tri, qs, bias_rows, tri_page, pages_mat, cache_k, cache_v)


def _layer_norm_silu(y, g, b):
    mu = jnp.mean(y, axis=-1, keepdims=True)
    yc = y - mu
    var = jnp.mean(yc * yc, axis=-1, keepdims=True)
    y = yc * lax.rsqrt(var + EPS) * g + b
    return y * jax.nn.sigmoid(y)


def _pool_conv_prompt_kernel(u_ref, c_ref, wp_ref, ps_ref, wdw_ref, bdw_ref, lng_ref, lnb_ref, pw_ref,
                             o_ref, us, cs, ys, part, zs):
    tl = SEQ_TILE
    ti = pl.program_id(1)

    @pl.when(ti == 0)
    def _():
        us[0:HALO] = jnp.zeros((HALO, POOL_WIDTH), F32)
        cs[0:HALO] = jnp.zeros((HALO, CONV_WIDTH), F32)

    @pl.when(ti > 0)
    def _():
        us[0:HALO] = us[tl:tl + HALO]
        cs[0:HALO] = cs[tl:tl + HALO]

    u = u_ref[0]
    us[HALO:HALO + tl] = u
    cs[HALO:HALO + tl] = c_ref[0]

    end = HALO + tl
    part[0, 8:end] = us[8:end] + us[7:end - 1]
    part[1, 16:end] = part[0, 16:end] + part[0, 14:end - 2]
    part[2, 24:end] = part[1, 24:end] + part[1, 20:end - 4]
    s2 = part[0, HALO:end, 0:LANES]
    s4 = part[1, HALO:end, 0:LANES]
    s8 = part[2, HALO:end, LANES:]
    s16 = s8 + part[2, HALO - 8:end - 8, LANES:]
    low_group = lax.broadcasted_iota(jnp.int32, (tl, LANES), 1) < POOL_GROUP
    seen = ti * tl + lax.broadcasted_iota(jnp.int32, (tl, LANES), 0) + 1
    cnt_lo = jnp.minimum(jnp.where(low_group, POOL_WINDOWS[0], POOL_WINDOWS[1]), seen).astype(F32)
    cnt_hi = jnp.minimum(jnp.where(low_group, POOL_WINDOWS[2], POOL_WINDOWS[3]), seen).astype(F32)
    mean = jnp.concatenate([jnp.where(low_group, s2, s4) / cnt_lo,
                            jnp.where(low_group, s8, s16) / cnt_hi], axis=1)
    pool = _dot((mean - u).astype(BF16), wp_ref[...]) * ps_ref[...]

    lead = HALO - CONV_STATE
    for r0 in range(0, tl, CONV_CHUNK):
        acc = jnp.broadcast_to(bdw_ref[...], (CONV_CHUNK, CONV_WIDTH))
        for phase in range(SUBLANES):
            rows = CONV_CHUNK if phase == 0 else CONV_CHUNK + SUBLANES
            z = None
            for k in range(CONV_K):
                if (k + lead) % SUBLANES == phase:
                    term = cs[pl.ds(r0 + k + lead - phase, rows), :] * wdw_ref[k:k + 1, :]
                    z = term if z is None else z + term
            if phase == 0:
                acc = acc + z
            else:
                zs[phase - 1] = z
                acc = acc + zs[phase - 1, pl.ds(phase, CONV_CHUNK), :]
        ys[r0:r0 + CONV_CHUNK] = acc
    conv = _dot(_layer_norm_silu(ys[...], lng_ref[...], lnb_ref[...]).astype(BF16), pw_ref[...])
    o_ref[0] = jnp.concatenate([pool, conv], axis=1).astype(BF16)


def _pool_conv_prompt(u, c, wp, ps, wdw, bdw, lng, lnb, pw):
    b, seq, _ = u.shape
    tl = SEQ_TILE
    vec = _const_spec((1, CONV_WIDTH))
    return pl.pallas_call(
        _pool_conv_prompt_kernel,
        grid=(b, seq // tl),
        in_specs=[pl.BlockSpec((1, tl, POOL_WIDTH), lambda bi, i: (bi, i, 0)),
                  pl.BlockSpec((1, tl, CONV_WIDTH), lambda bi, i: (bi, i, 0)),
                  _const_spec(wp.shape), vec, _const_spec(wdw.shape), vec, vec, vec, _const_spec(pw.shape)],
        out_specs=pl.BlockSpec((1, tl, POOL_WIDTH + CONV_WIDTH), lambda bi, i: (bi, i, 0)),
        out_shape=jax.ShapeDtypeStruct((b, seq, POOL_WIDTH + CONV_WIDTH), BF16),
        scratch_shapes=[pltpu.VMEM((HALO + tl, POOL_WIDTH), F32), pltpu.VMEM((HALO + tl, CONV_WIDTH), F32),
                        pltpu.VMEM((tl, CONV_WIDTH), F32), pltpu.VMEM((3, HALO + tl, POOL_WIDTH), F32),
                        pltpu.VMEM((SUBLANES - 1, CONV_CHUNK + SUBLANES, CONV_WIDTH), F32)],
        compiler_params=pltpu.CompilerParams(dimension_semantics=("arbitrary", "arbitrary")),
        name="pool_conv_prompt",
    )(u, c, wp, ps, wdw, bdw, lng, lnb, pw)


def _pool_conv_sample_kernel(u_ref, c_ref, pprev_ref, cprev_ref, pwin_ref, wp_ref, ps_ref, wdw_ref, bdw_ref,
                             lng_ref, lnb_ref, pw_ref, o_ref):
    u = u_ref[...]
    c = c_ref[...]
    mean = u * pwin_ref[POOL_STATE:POOL_STATE + 1, :]
    for k in range(POOL_STATE):
        mean = mean + pprev_ref[k] * pwin_ref[k:k + 1, :]
    pool = _dot((mean - u).astype(BF16), wp_ref[...]) * ps_ref[...]
    y = c * wdw_ref[CONV_STATE:CONV_K, :] + bdw_ref[...]
    for k in range(CONV_STATE):
        y = y + cprev_ref[k] * wdw_ref[k:k + 1, :]
    conv = _dot(_layer_norm_silu(y, lng_ref[...], lnb_ref[...]).astype(BF16), pw_ref[...])
    o_ref[...] = jnp.concatenate([pool, conv], axis=1).astype(BF16)


def _pool_conv_sample(layer, u, c, pool_rows, conv_rows, pwin, wp, ps, wdw, bdw, lng, lnb, pw):
    bs = u.shape[0]
    tb = 32
    vec = _const_spec((1, CONV_WIDTH))
    return pl.pallas_call(
        _pool_conv_sample_kernel,
        grid=(bs // tb,),
        in_specs=[pl.BlockSpec((tb, POOL_WIDTH), lambda i: (i, 0)),
                  pl.BlockSpec((tb, CONV_WIDTH), lambda i: (i, 0)),
                  pl.BlockSpec((None, POOL_STATE, tb, POOL_WIDTH), lambda i: (layer, 0, i, 0)),
                  pl.BlockSpec((None, CONV_STATE, tb, CONV_WIDTH), lambda i: (layer, 0, i, 0)),
                  _const_spec(pwin.shape), _const_spec(wp.shape), vec, _const_spec(wdw.shape), vec, vec, vec,
                  _const_spec(pw.shape)],
        out_specs=pl.BlockSpec((tb, POOL_WIDTH + CONV_WIDTH), lambda i: (i, 0)),
        out_shape=jax.ShapeDtypeStruct((bs, POOL_WIDTH + CONV_WIDTH), BF16),
        name="pool_conv_sample",
    )(u, c, pool_rows, conv_rows, pwin, wp, ps, wdw, bdw, lng, lnb, pw)


def _out_proj_kernel(x_ref, a_ref, bc_ref, wo_ref, g_ref, wq_ref, x1_ref, q_ref):
    x1 = x_ref[...] + _dot(a_ref[...], wo_ref[0:SB_WIDTH, :]) + _dot(bc_ref[...], wo_ref[SB_WIDTH:, :])
    x1_ref[...] = x1
    h = _rms(x1, g_ref[...]).astype(BF16)
    q_ref[...] = (_dot(h, wq_ref[...]) * MEM_SCALE).astype(BF16)


def _out_proj(x, a, bc, wo, g, wq):
    rows = x.shape[0]
    tm = min(ROW_TILE, rows)

    def row_spec(width):
        return pl.BlockSpec((tm, width), lambda i: (i, 0))

    return pl.pallas_call(
        _out_proj_kernel,
        grid=(rows // tm,),
        in_specs=[row_spec(D_MODEL), row_spec(SB_WIDTH), row_spec(POOL_WIDTH + CONV_WIDTH),
                  _const_spec(wo.shape), _const_spec((1, D_MODEL)), _const_spec(wq.shape)],
        out_specs=[row_spec(D_MODEL), row_spec(D_MODEL)],
        out_shape=[jax.ShapeDtypeStruct((rows, D_MODEL), F32), jax.ShapeDtypeStruct((rows, D_MODEL), BF16)],
        name="out_proj",
    )(x, a, bc, wo, g, wq)


def _mem_attn_prompt_kernel(q_ref, mk_ref, mv_ref, o_ref):
    outs = []
    for h in range(MEM_HEADS):
        hs = slice(h * MEM_HEAD_DIM, (h + 1) * MEM_HEAD_DIM)
        s = _dot_nt(q_ref[0, :, hs], mk_ref[:, hs])
        e = jnp.exp(s - jnp.max(s, axis=-1, keepdims=True))
        den = jnp.sum(e, axis=-1, keepdims=True)
        outs.append(_dot((e / den).astype(BF16), mv_ref[:, hs]))
    o_ref[0] = jnp.concatenate(outs, axis=1).astype(BF16)


def _mem_attn_prompt(layer, q, mk, mv):
    b, seq, _ = q.shape
    tm = ROW_TILE
    mem_len = mk.shape[2]
    m_spec = pl.BlockSpec((None, None, mem_len, D_MODEL), lambda bi, i: (layer, bi, 0, 0))
    qo_spec = pl.BlockSpec((1, tm, D_MODEL), lambda bi, i: (bi, i, 0))
    return pl.pallas_call(
        _mem_attn_prompt_kernel,
        grid=(b, seq // tm),
        in_specs=[qo_spec, m_spec, m_spec],
        out_specs=qo_spec,
        out_shape=jax.ShapeDtypeStruct((b, seq, D_MODEL), BF16),
        name="mem_attn_prompt",
    )(q, mk, mv)


def _mem_attn_one(q, mk_ref, mv_ref):
    q = q.astype(F32)
    s = jnp.sum(mk_ref[...] * q[None], axis=-1, keepdims=True)
    e = jnp.exp(s - jnp.max(s, axis=0, keepdims=True))
    return jnp.sum(e * mv_ref[...], axis=0) / jnp.sum(e, axis=0)


def _mo_ffn_kernel(final, mem_per_step, x1_ref, o_ref, wmo_ref, g_ref, wup_ref, wdn_ref, *refs):
    refs = list(refs)
    gf_ref = refs.pop(0) if final else None
    if mem_per_step:
        qs_ref, mk_ref, mv_ref = refs[:3]
        out_ref, os_ref = refs[3:]
        for r in range(mem_per_step):
            os_ref[r] = _mem_attn_one(qs_ref[r], mk_ref.at[r], mv_ref.at[r]).astype(BF16)
    else:
        (out_ref,) = refs
    x2 = x1_ref[...] + _dot(o_ref[...], wmo_ref[...])
    h = _rms(x2, g_ref[...]).astype(BF16)
    acc = x2
    for c0 in range(0, D_FF, FF_CHUNK):
        up = jnp.maximum(_dot(h, wup_ref[:, c0:c0 + FF_CHUNK]), 0.0)
        acc = acc + _dot((up * up).astype(BF16), wdn_ref[c0:c0 + FF_CHUNK, :])
    if final:
        acc = _rms(acc, gf_ref[...])
    out_ref[...] = acc


def _mo_ffn(x1, o, wmo, g, wup, wdn, g_final=None, mem=None):
    rows = x1.shape[0]
    tm = min(ROW_TILE, rows)
    steps = rows // tm
    row_spec = pl.BlockSpec((tm, D_MODEL), lambda i: (i, 0))
    final = g_final is not None
    in_specs = [row_spec, row_spec, _const_spec(wmo.shape), _const_spec((1, D_MODEL)), _const_spec(wup.shape),
                _const_spec(wdn.shape)]
    args = [x1, o, wmo, g, wup, wdn]
    out_specs = [row_spec]
    out_shape = [jax.ShapeDtypeStruct((rows, D_MODEL), F32)]
    if final:
        in_specs.append(_const_spec((1, D_MODEL)))
        args.append(g_final)
    per_step = 0
    if mem is not None:
        layer, qs, cache_k, cache_v = mem
        bs, mem_len = qs.shape[0], cache_k.shape[2]
        assert bs % steps == 0
        per_step = bs // steps
        vec_spec = pl.BlockSpec((per_step, MEM_HEADS, MEM_HEAD_DIM), lambda i: (i, 0, 0))
        m_spec = pl.BlockSpec((None, per_step, mem_len, MEM_HEADS, MEM_HEAD_DIM), lambda i: (layer, i, 0, 0, 0))
        in_specs += [vec_spec, m_spec, m_spec]
        args += [qs, cache_k, cache_v]
        out_specs.append(vec_spec)
        out_shape.append(jax.ShapeDtypeStruct((bs, MEM_HEADS, MEM_HEAD_DIM), BF16))
    outs = pl.pallas_call(
        functools.partial(_mo_ffn_kernel, final, per_step),
        grid=(steps,),
        in_specs=in_specs,
        out_specs=out_specs,
        out_shape=out_shape,
        name="mo_ffn",
    )(*args)
    return outs if mem is not None else outs[0]


def _suffix_triangle(n):
    j = lax.broadcasted_iota(jnp.int32, (n, n), 0)
    s = lax.broadcasted_iota(jnp.int32, (n, n), 1)
    return jnp.where(j > s, -1.0, 0.0).astype(BF16)


def _later_pages_matrix(n_pages):
    n = n_pages * SB_HEADS
    r = lax.broadcasted_iota(jnp.int32, (n, n), 0)
    c = lax.broadcasted_iota(jnp.int32, (n, n), 1)
    same_head = (r % SB_HEADS) == (c % SB_HEADS)
    return jnp.where(same_head & (c // SB_HEADS > r // SB_HEADS), -1.0, 0.0).astype(BF16)


def _pool_window_rows():
    row = lax.broadcasted_iota(jnp.int32, (POOL_STATE + 1, POOL_WIDTH), 0)
    ch = lax.broadcasted_iota(jnp.int32, (POOL_STATE + 1, POOL_WIDTH), 1)
    w = jnp.zeros((POOL_STATE + 1, POOL_WIDTH), jnp.int32)
    for g, win in enumerate(POOL_WINDOWS):
        w = jnp.where(ch // POOL_GROUP == g, win, w)
    return jnp.where(row >= POOL_STATE + 1 - w, 1.0 / w.astype(F32), 0.0)


def _block_diag(w_grp):
    n, g, _ = w_grp.shape
    out = jnp.zeros((n * g, n * g), w_grp.dtype)
    for i in range(n):
        out = lax.dynamic_update_slice(out, w_grp[i], (i * g, i * g))
    return out


def kernel(x_prompt, x_sample, cache_sb_k, cache_sb_v, cache_mem_k, cache_mem_v, state_pool, state_conv,
           page_table, mem_prompt, norm_mix_g, w_in, sb_bias, pool_w, pool_scale, conv_w, conv_b, conv_ln_g,
           conv_ln_b, conv_pw, w_out, norm_mem_g, w_mq, w_mk, w_mv, w_mo, norm_ffn_g, w_up, w_down,
           norm_final_g):
    bp, seq, _ = x_prompt.shape
    bs = x_sample.shape[0]
    n_pages = page_table.shape[1]
    n_phys = cache_sb_k.shape[1]
    mem_len = mem_prompt.shape[1]
    assert x_sample.shape[1] == 1 and seq % SB_TILE == 0 and seq >= CONV_STATE

    cache_k = cache_sb_k.transpose(0, 1, 3, 4, 2).reshape(DEPTH, n_phys, SB_WIDTH, PAGE_SIZE)
    cache_v = cache_sb_v.transpose(0, 1, 3, 4, 2).reshape(DEPTH, n_phys, SB_WIDTH, PAGE_SIZE)
    pool_rows = state_pool.transpose(0, 2, 1, 3)
    conv_rows = state_conv.transpose(0, 2, 1, 3)

    tri_prompt = _suffix_triangle(SB_TILE)
    tri_page = _suffix_triangle(PAGE_SIZE)
    pages_mat = _later_pages_matrix(n_pages)
    pwin = _pool_window_rows()

    mk_all, mv_all, mkb_all, mvb_all = _mem_kv(mem_prompt.reshape(bp * mem_len, D_MODEL),
                                                w_mk.astype(BF16), w_mv.astype(BF16))
    mkb_all = mkb_all.reshape(DEPTH, bp, mem_len, D_MODEL)
    mvb_all = mvb_all.reshape(DEPTH, bp, mem_len, D_MODEL)

    def heads_last(t):
        return t.reshape(t.shape[0], SB_HEADS, SB_HEAD_DIM, t.shape[2]).transpose(0, 3, 1, 2)

    xp = x_prompt
    xs = x_sample
    sbk_p, sbv_p, sbk_s, sbv_s = [], [], [], []
    pool_p, pool_s, conv_p, conv_s = [], [], [], []
    for l in range(DEPTH):
        last = l == DEPTH - 1
        g_mix = norm_mix_g[l].reshape(1, D_MODEL)
        g_mem = norm_mem_g[l].reshape(1, D_MODEL)
        g_ffn = norm_ffn_g[l].reshape(1, D_MODEL)
        g_fin = norm_final_g.reshape(1, D_MODEL) if last else None
        wq_sb = w_in[l, :, 0:SB_WIDTH].astype(BF16)
        wkt = w_in[l, :, SB_WIDTH:2 * SB_WIDTH].T.astype(BF16)
        wvt = w_in[l, :, 2 * SB_WIDTH:3 * SB_WIDTH].T.astype(BF16)
        wr = w_in[l, :, 3 * SB_WIDTH:].astype(BF16)
        wo, wq, wmo = (w.astype(BF16) for w in (w_out[l], w_mq[l], w_mo[l]))
        wup, wdn = w_up[l].astype(BF16), w_down[l].astype(BF16)
        wp = _block_diag(pool_w[l]).astype(BF16)
        ps = pool_scale[l].reshape(1, POOL_WIDTH)
        bdw = conv_b[l].reshape(1, CONV_WIDTH)
        lng = conv_ln_g[l].reshape(1, CONV_WIDTH)
        lnb = conv_ln_b[l].reshape(1, CONV_WIDTH)
        pw = conv_pw[l].astype(BF16)
        bias2 = sb_bias[l] * LOG2E
        bias_rows = jnp.broadcast_to(jnp.tile(bias2, n_pages)[:, None], (n_pages * SB_HEADS, LANES))

        q, kt, vt, ktb, vtb, u, c = _in_proj(xp.reshape(bp, seq, D_MODEL), g_mix, wq_sb, wkt, wvt, wr)
        sbk_p.append(heads_last(kt))
        sbv_p.append(heads_last(vt))
        pool_p.append(u[:, seq - POOL_STATE:])
        conv_p.append(c[:, seq - CONV_STATE:])
        q_s, kt_s, vt_s, _, _, u_s, c_s = _in_proj(xs.reshape(1, bs, D_MODEL), g_mix, wq_sb, wkt, wvt, wr)
        sbk_s.append(heads_last(kt_s).reshape(bs, 1, SB_HEADS, SB_HEAD_DIM))
        sbv_s.append(heads_last(vt_s).reshape(bs, 1, SB_HEADS, SB_HEAD_DIM))
        u_s, c_s = u_s[0], c_s[0]
        pool_s.append(jnp.concatenate([state_pool[l, :, 1:], u_s[:, None]], axis=1))
        conv_s.append(jnp.concatenate([state_conv[l, :, 1:], c_s[:, None]], axis=1))
        a, a_s = _sb_attention(l, bias2, q, ktb, vtb, tri_prompt, page_table, q_s.reshape(bs, 1, SB_WIDTH),
                               bias_rows, tri_page, pages_mat, cache_k, cache_v)

        bc_s = _pool_conv_sample(l, u_s, c_s, pool_rows, conv_rows, pwin, wp, ps, conv_w[l], bdw, lng, lnb, pw)
        x1_s, qm_s = _out_proj(xs.reshape(bs, D_MODEL), a_s.reshape(bs, SB_WIDTH).astype(BF16), bc_s, wo, g_mem, wq)

        bc = _pool_conv_prompt(u, c, wp, ps, conv_w[l], bdw, lng, lnb, pw)
        x1, qm = _out_proj(xp.reshape(bp * seq, D_MODEL), a.reshape(bp * seq, SB_WIDTH),
                           bc.reshape(bp * seq, POOL_WIDTH + CONV_WIDTH), wo, g_mem, wq)
        o = _mem_attn_prompt(l, qm.reshape(bp, seq, D_MODEL), mkb_all, mvb_all)
        xp, o_s = _mo_ffn(x1, o.reshape(bp * seq, D_MODEL), wmo, g_ffn, wup, wdn, g_fin,
                          mem=(l, qm_s.reshape(bs, MEM_HEADS, MEM_HEAD_DIM), cache_mem_k, cache_mem_v))
        xs = _mo_ffn(x1_s, o_s.reshape(bs, D_MODEL), wmo, g_ffn, wup, wdn, g_fin)

    return (xp.reshape(bp, seq, D_MODEL), xs.reshape(bs, 1, D_MODEL),
            jnp.stack(sbk_p), jnp.stack(sbv_p), jnp.stack(sbk_s), jnp.stack(sbv_s),
            mk_all.reshape(DEPTH, bp, mem_len, MEM_HEADS, MEM_HEAD_DIM),
            mv_all.reshape(DEPTH, bp, mem_len, MEM_HEADS, MEM_HEAD_DIM),
            jnp.stack(pool_p), jnp.stack(pool_s), jnp.stack(conv_p), jnp.stack(conv_s))
```

```python
import functools

import jax
import jax.numpy as jnp
from jax import lax
from jax.experimental import pallas as pl
from jax.experimental.pallas import tpu as pltpu

F32 = jnp.float32
BF16 = jnp.bfloat16

D_MODEL = 1024
DEPTH = 2
PAGE_SIZE = 128
SB_HEAD_DIM = 64
SB_WIDTH = D_MODEL // 2
SB_HEADS = SB_WIDTH // SB_HEAD_DIM
POOL_WINDOWS = (2, 4, 8, 16)
POOL_WIDTH = D_MODEL // 4
POOL_GROUP = POOL_WIDTH // len(POOL_WINDOWS)
POOL_STATE = max(POOL_WINDOWS) - 1
CONV_WIDTH = D_MODEL // 4
CONV_K = 31
CONV_STATE = CONV_K - 1
MEM_HEADS = 4
MEM_HEAD_DIM = D_MODEL // MEM_HEADS
D_FF = 4 * D_MODEL
EPS = 1e-6

SB_SCALE = SB_HEAD_DIM ** -0.5
LOG2E = 1.4426950408889634
MEM_SCALE = MEM_HEAD_DIM ** -0.5

LANES = 128
SUBLANES = 8
SB_TILE = 256
ROW_TILE = 512
SEQ_TILE = 512
HALO = 32
CONV_CHUNK = 64
FF_CHUNK = 1024
MEM_CHUNK = 32


def _const_spec(shape):
    nd = len(shape)
    return pl.BlockSpec(shape, lambda *_: (0,) * nd, pipeline_mode=pl.Buffered(1))


def _rms(x, g):
    return x * lax.rsqrt(jnp.mean(x * x, axis=-1, keepdims=True) + EPS) * g


def _softplus2(z):
    return jnp.maximum(z, 0.0) + jnp.log(1.0 + jnp.exp2(-jnp.abs(z))) * LOG2E


def _dot(a, b):
    return jnp.dot(a, b, preferred_element_type=F32)


def _dot_nt(a, b):
    return lax.dot_general(a, b, (((1,), (1,)), ((), ())), preferred_element_type=F32)


def _mem_kv_kernel(m_ref, wk_ref, wv_ref, k_ref, v_ref, kb_ref, vb_ref):
    m = m_ref[...].astype(BF16)
    k = _dot(m, wk_ref[...])
    v = _dot(m, wv_ref[...])
    k_ref[...] = k
    v_ref[...] = v
    kb_ref[...] = k.astype(BF16)
    vb_ref[...] = v.astype(BF16)


def _mem_kv(mem_rows, wk, wv):
    rows = mem_rows.shape[0]
    tm = min(ROW_TILE, rows)
    w_spec = pl.BlockSpec((None, D_MODEL, D_MODEL), lambda l, i: (l, 0, 0))
    o_spec = pl.BlockSpec((None, tm, D_MODEL), lambda l, i: (l, i, 0))
    f = jax.ShapeDtypeStruct((DEPTH, rows, D_MODEL), F32)
    b = jax.ShapeDtypeStruct((DEPTH, rows, D_MODEL), BF16)
    return pl.pallas_call(
        _mem_kv_kernel,
        grid=(DEPTH, rows // tm),
        in_specs=[pl.BlockSpec((tm, D_MODEL), lambda l, i: (i, 0)), w_spec, w_spec],
        out_specs=[o_spec, o_spec, o_spec, o_spec],
        out_shape=[f, f, b, b],
        name="mem_kv",
    )(mem_rows, wk, wv)


def _in_proj_kernel(x_ref, g_ref, wq_ref, wkt_ref, wvt_ref, wr_ref, q_ref, kt_ref, vt_ref, ktb_ref, vtb_ref,
                    u_ref, c_ref):
    h = _rms(x_ref[...], g_ref[...]).astype(BF16)
    q_ref[...] = (_dot(h, wq_ref[...]) * (SB_SCALE * LOG2E)).astype(BF16)
    kt = _dot_nt(wkt_ref[...], h)
    kt_ref[...] = kt
    ktb_ref[...] = kt.astype(BF16)
    vt = _dot_nt(wvt_ref[...], h)
    vt_ref[...] = vt
    vtb_ref[...] = vt.astype(BF16)
    u_ref[...] = _dot(h, wr_ref[:, 0:POOL_WIDTH])
    glu_a = _dot(h, wr_ref[:, POOL_WIDTH:POOL_WIDTH + CONV_WIDTH])
    glu_g = _dot(h, wr_ref[:, POOL_WIDTH + CONV_WIDTH:])
    c_ref[...] = glu_a * jax.nn.sigmoid(glu_g)


def _in_proj(x, g, wq, wkt, wvt, wr):
    b, seq, _ = x.shape
    tm = min(ROW_TILE, seq)

    def row_spec(width):
        return pl.BlockSpec((None, tm, width), lambda bi, i: (bi, i, 0))

    t_spec = pl.BlockSpec((None, SB_WIDTH, tm), lambda bi, i: (bi, 0, i))

    def rows(width, dtype):
        return jax.ShapeDtypeStruct((b, seq, width), dtype)

    def cols(dtype):
        return jax.ShapeDtypeStruct((b, SB_WIDTH, seq), dtype)

    return pl.pallas_call(
        _in_proj_kernel,
        grid=(b, seq // tm),
        in_specs=[row_spec(D_MODEL), _const_spec((1, D_MODEL)), _const_spec(wq.shape), _const_spec(wkt.shape),
                  _const_spec(wvt.shape), _const_spec(wr.shape)],
        out_specs=[row_spec(SB_WIDTH), t_spec, t_spec, t_spec, t_spec, row_spec(POOL_WIDTH),
                   row_spec(CONV_WIDTH)],
        out_shape=[rows(SB_WIDTH, BF16), cols(F32), cols(F32), cols(BF16), cols(BF16), rows(POOL_WIDTH, F32),
                   rows(CONV_WIDTH, F32)],
        name="in_proj",
    )(x, g, wq, wkt, wvt, wr)


def _sb_kernel(layer, n_pages, per_step, grid, pt_ref, bias_ref, q_ref, k_ref, v_ref, tri_ref, qs_ref, brow_ref,
               tpage_ref, pages_ref, ck_ref, cv_ref, o_ref, os_ref, acc_ref, run_ref, sp_ref, lsz_ref, ws_ref,
               page_buf, page_sem):
    n_items = 2 * per_step * n_pages
    n_steps = grid[0] * grid[1]
    step = pl.program_id(0) * grid[1] + pl.program_id(1)
    slot = lax.rem(step, 2)

    def page_copy(for_step, to_slot, n):
        r, kv, p = n // (2 * n_pages), (n // n_pages) % 2, n % n_pages
        cache = cv_ref if kv else ck_ref
        page = pt_ref[for_step * per_step + r, p]
        return pltpu.make_async_copy(cache.at[layer, page], page_buf.at[to_slot, n], page_sem.at[to_slot])

    @pl.when(step == 0)
    def _():
        for n in range(n_items):
            page_copy(0, 0, n).start()

    for n in range(n_items):
        page_copy(step, slot, n).wait()
    next_step = lax.rem(step + 1, n_steps)

    def prefetch(h):
        for n in range(h * n_items // SB_HEADS, (h + 1) * n_items // SB_HEADS):
            page_copy(next_step, 1 - slot, n).start()

    page_refs = [page_buf.at[slot, n] for n in range(n_items)]

    rows = n_pages * SB_HEADS
    head_of_lane = lax.broadcasted_iota(jnp.int32, (SB_HEADS, SB_WIDTH), 1) // SB_HEAD_DIM
    own = head_of_lane == lax.broadcasted_iota(jnp.int32, (SB_HEADS, SB_WIDTH), 0)
    qs_heads = [jnp.where(own, jnp.broadcast_to(qs_ref[r].astype(F32), (SB_HEADS, SB_WIDTH)), 0.0).astype(BF16)
                for r in range(per_step)]
    items = [(r, p) for r in range(per_step) for p in range(n_pages)]
    z_parts = [None] * len(items)

    def item_slice(h):
        return range(h * len(items) // SB_HEADS, (h + 1) * len(items) // SB_HEADS)

    def sample_scores(h):
        prefetch(h)
        for n in item_slice(h):
            r, p = items[n]
            page = page_refs[2 * r * n_pages + p]
            z_parts[n] = _dot(qs_heads[r], page[...].astype(BF16)) + brow_ref[0:SB_HEADS, :]

    def sample_weights():
        z = jnp.concatenate(z_parts, axis=0)
        sp = _softplus2(z)
        sp_b = sp.astype(BF16)
        later_in_page = _dot(sp_b, tpage_ref[...])
        later_pages = jnp.concatenate(
            [jnp.sum(_dot(pages_ref[...], sp_b[r * rows:(r + 1) * rows]), axis=1, keepdims=True)
             for r in range(per_step)], axis=0)
        ws_ref[...] = jnp.exp2((z - sp) + later_in_page + later_pages)

    def sample_values(h):
        for r in range(per_step):
            acc = jnp.zeros((SB_HEAD_DIM, PAGE_SIZE), F32)
            for p in range(n_pages):
                page = page_refs[(2 * r + 1) * n_pages + p]
                w_row = ws_ref[pl.ds(r * rows + p * SB_HEADS + h, 1), :]
                acc = acc + page[h * SB_HEAD_DIM:(h + 1) * SB_HEAD_DIM, :] * w_row
            os_ref[r, h * SB_HEAD_DIM:(h + 1) * SB_HEAD_DIM, :] = jnp.sum(acc, axis=1, keepdims=True)

    t = SB_TILE
    i = pl.program_id(1)
    first_head = lax.broadcasted_iota(jnp.int32, (t, LANES), 1) < SB_HEAD_DIM
    q_heads = []
    for p in range(SB_HEADS // 2):
        qp = q_ref[0, :, p * LANES:(p + 1) * LANES].astype(F32)
        q_heads += [jnp.where(first_head, qp, 0.0).astype(BF16), jnp.where(first_head, 0.0, qp).astype(BF16)]
    visible = (lax.broadcasted_iota(jnp.int32, (t, t), 1) < lax.broadcasted_iota(jnp.int32, (t, t), 0))
    acc_ref[...] = jnp.zeros_like(acc_ref)
    run_ref[...] = jnp.zeros_like(run_ref)

    def pair_rows(h):
        return slice((h // 2) * LANES, (h // 2 + 1) * LANES)

    def scores(j, masked, between=None):
        start = pl.multiple_of(j * t, t)
        for h in range(SB_HEADS):
            if between is not None:
                between(h)
            z = _dot(q_heads[h], k_ref[0, pair_rows(h), pl.ds(start, t)]) + bias_ref[h]
            sp = _softplus2(z)
            lsz = z - sp
            if masked:
                sp = jnp.where(visible, sp, 0.0)
                lsz = jnp.where(visible, lsz, -jnp.inf)
            sp_ref[h] = sp.astype(BF16)
            lsz_ref[h] = lsz

    def weights(j, between=None):
        start = pl.multiple_of(j * t, t)
        for h in range(SB_HEADS):
            if between is not None:
                between(h)
            sp = sp_ref[h]
            later = _dot(sp, tri_ref[...])
            run = run_ref[h]
            w = jnp.exp2(lsz_ref[h] + later + jnp.concatenate([run, run], axis=1))
            acc_ref[h] += _dot_nt(w.astype(BF16), v_ref[0, pair_rows(h), pl.ds(start, t)])
            run_ref[h] = run + (later[:, :1] - sp[:, :LANES].astype(F32)[:, :1])

    scores(i, True, between=sample_scores)
    sample_weights()

    def key_tile_pair(jj, carry):
        weights(i - jj)
        scores(i - 1 - jj, False)
        return carry

    lax.fori_loop(0, i, key_tile_pair, 0)
    weights(0, between=sample_values)

    @pl.when(step == n_steps - 1)
    def _():
        for n in range(n_items):
            page_copy(next_step, 1 - slot, n).wait()

    for p in range(SB_HEADS // 2):
        o_ref[0, :, p * LANES:(p + 1) * LANES] = jnp.where(first_head, acc_ref[2 * p], acc_ref[2 * p + 1]).astype(BF16)


def _sb_attention(layer, bias2, q, kt, vt, tri, page_table, qs, bias_rows, tri_page, pages_mat, cache_k, cache_v):
    b, seq, _ = q.shape
    t = SB_TILE
    nq = seq // t
    bs, n_pages = page_table.shape
    assert bs % (b * nq) == 0
    per_step = bs // (b * nq)
    rows = n_pages * SB_HEADS
    kv_spec = pl.BlockSpec((1, SB_WIDTH, seq), lambda bi, i, pt: (bi, 0, 0))
    qo_spec = pl.BlockSpec((1, t, SB_WIDTH), lambda bi, i, pt: (bi, i, 0))
    sv_spec = pl.BlockSpec((per_step, 1, SB_WIDTH), lambda bi, i, pt: (bi * nq + i, 0, 0))

    hbm = pl.BlockSpec(memory_space=pl.ANY)
    grid_spec = pltpu.PrefetchScalarGridSpec(
        num_scalar_prefetch=1,
        grid=(b, nq),
        in_specs=[pl.BlockSpec(memory_space=pltpu.SMEM), qo_spec, kv_spec, kv_spec, _const_spec((t, t)), sv_spec,
                  _const_spec((rows, LANES)), _const_spec((PAGE_SIZE, PAGE_SIZE)), _const_spec((rows, rows)), hbm, hbm],
        out_specs=[qo_spec, pl.BlockSpec((per_step, SB_WIDTH, 1), lambda bi, i, pt: (bi * nq + i, 0, 0))],
        scratch_shapes=[pltpu.VMEM((SB_HEADS, t, LANES), F32), pltpu.VMEM((SB_HEADS, t, LANES), F32),
                        pltpu.VMEM((SB_HEADS, t, t), BF16), pltpu.VMEM((SB_HEADS, t, t), F32),
                        pltpu.VMEM((per_step * rows, PAGE_SIZE), F32),
                        pltpu.VMEM((2, 2 * per_step * n_pages, SB_WIDTH, PAGE_SIZE), F32),
                        pltpu.SemaphoreType.DMA((2,))],
    )
    return pl.pallas_call(
        functools.partial(_sb_kernel, layer, n_pages, per_step, (b, nq)),
        grid_spec=grid_spec,
        out_shape=[jax.ShapeDtypeStruct((b, seq, SB_WIDTH), BF16), jax.ShapeDtypeStruct((bs, SB_WIDTH, 1), F32)],
        compiler_params=pltpu.CompilerParams(dimension_semantics=("arbitrary", "arbitrary")),
        name="sb_attention",
    )(page_table, bias2, q, kt, vt, tri, qs, bias_rows, tri_page, pages_mat, cache_k, cache_v)


def _layer_norm_silu(y, g, b):
    mu = jnp.mean(y, axis=-1, keepdims=True)
    yc = y - mu
    var = jnp.mean(yc * yc, axis=-1, keepdims=True)
    y = yc * lax.rsqrt(var + EPS) * g + b
    return y * jax.nn.sigmoid(y)


def _pool_conv_prompt_kernel(u_ref, c_ref, wp_ref, ps_ref, wdw_ref, bdw_ref, lng_ref, lnb_ref, pw_ref,
                             o_ref, us, cs, ys, part, zs):
    tl = SEQ_TILE
    ti = pl.program_id(1)

    @pl.when(ti == 0)
    def _():
        us[0:HALO] = jnp.zeros((HALO, POOL_WIDTH), F32)
        cs[0:HALO] = jnp.zeros((HALO, CONV_WIDTH), F32)

    @pl.when(ti > 0)
    def _():
        us[0:HALO] = us[tl:tl + HALO]
        cs[0:HALO] = cs[tl:tl + HALO]

    u = u_ref[0]
    us[HALO:HALO + tl] = u
    cs[HALO:HALO + tl] = c_ref[0]

    end = HALO + tl
    part[0, 8:end] = us[8:end] + us[7:end - 1]
    part[1, 16:end] = part[0, 16:end] + part[0, 14:end - 2]
    part[2, 24:end] = part[1, 24:end] + part[1, 20:end - 4]
    s2 = part[0, HALO:end, 0:LANES]
    s4 = part[1, HALO:end, 0:LANES]
    s8 = part[2, HALO:end, LANES:]
    s16 = s8 + part[2, HALO - 8:end - 8, LANES:]
    low_group = lax.broadcasted_iota(jnp.int32, (tl, LANES), 1) < POOL_GROUP
    seen = ti * tl + lax.broadcasted_iota(jnp.int32, (tl, LANES), 0) + 1
    cnt_lo = jnp.minimum(jnp.where(low_group, POOL_WINDOWS[0], POOL_WINDOWS[1]), seen).astype(F32)
    cnt_hi = jnp.minimum(jnp.where(low_group, POOL_WINDOWS[2], POOL_WINDOWS[3]), seen).astype(F32)
    mean = jnp.concatenate([jnp.where(low_group, s2, s4) / cnt_lo,
                            jnp.where(low_group, s8, s16) / cnt_hi], axis=1)
    pool = _dot((mean - u).astype(BF16), wp_ref[...]) * ps_ref[...]

    lead = HALO - CONV_STATE
    for r0 in range(0, tl, CONV_CHUNK):
        acc = jnp.broadcast_to(bdw_ref[...], (CONV_CHUNK, CONV_WIDTH))
        for phase in range(SUBLANES):
            rows = CONV_CHUNK if phase == 0 else CONV_CHUNK + SUBLANES
            z = None
            for k in range(CONV_K):
                if (k + lead) % SUBLANES == phase:
                    term = cs[pl.ds(r0 + k + lead - phase, rows), :] * wdw_ref[k:k + 1, :]
                    z = term if z is None else z + term
            if phase == 0:
                acc = acc + z
            else:
                zs[phase - 1] = z
                acc = acc + zs[phase - 1, pl.ds(phase, CONV_CHUNK), :]
        ys[r0:r0 + CONV_CHUNK] = acc
    conv = _dot(_layer_norm_silu(ys[...], lng_ref[...], lnb_ref[...]).astype(BF16), pw_ref[...])
    o_ref[0] = jnp.concatenate([pool, conv], axis=1).astype(BF16)


def _pool_conv_prompt(u, c, wp, ps, wdw, bdw, lng, lnb, pw):
    b, seq, _ = u.shape
    tl = SEQ_TILE
    vec = _const_spec((1, CONV_WIDTH))
    return pl.pallas_call(
        _pool_conv_prompt_kernel,
        grid=(b, seq // tl),
        in_specs=[pl.BlockSpec((1, tl, POOL_WIDTH), lambda bi, i: (bi, i, 0)),
                  pl.BlockSpec((1, tl, CONV_WIDTH), lambda bi, i: (bi, i, 0)),
                  _const_spec(wp.shape), vec, _const_spec(wdw.shape), vec, vec, vec, _const_spec(pw.shape)],
        out_specs=pl.BlockSpec((1, tl, POOL_WIDTH + CONV_WIDTH), lambda bi, i: (bi, i, 0)),
        out_shape=jax.ShapeDtypeStruct((b, seq, POOL_WIDTH + CONV_WIDTH), BF16),
        scratch_shapes=[pltpu.VMEM((HALO + tl, POOL_WIDTH), F32), pltpu.VMEM((HALO + tl, CONV_WIDTH), F32),
                        pltpu.VMEM((tl, CONV_WIDTH), F32), pltpu.VMEM((3, HALO + tl, POOL_WIDTH), F32),
                        pltpu.VMEM((SUBLANES - 1, CONV_CHUNK + SUBLANES, CONV_WIDTH), F32)],
        compiler_params=pltpu.CompilerParams(dimension_semantics=("arbitrary", "arbitrary")),
        name="pool_conv_prompt",
    )(u, c, wp, ps, wdw, bdw, lng, lnb, pw)


def _pool_conv_sample_kernel(u_ref, c_ref, pprev_ref, cprev_ref, pwin_ref, wp_ref, ps_ref, wdw_ref, bdw_ref,
                             lng_ref, lnb_ref, pw_ref, o_ref):
    u = u_ref[...]
    c = c_ref[...]
    mean = u * pwin_ref[POOL_STATE:POOL_STATE + 1, :]
    for k in range(POOL_STATE):
        mean = mean + pprev_ref[k] * pwin_ref[k:k + 1, :]
    pool = _dot((mean - u).astype(BF16), wp_ref[...]) * ps_ref[...]
    y = c * wdw_ref[CONV_STATE:CONV_K, :] + bdw_ref[...]
    for k in range(CONV_STATE):
        y = y + cprev_ref[k] * wdw_ref[k:k + 1, :]
    conv = _dot(_layer_norm_silu(y, lng_ref[...], lnb_ref[...]).astype(BF16), pw_ref[...])
    o_ref[...] = jnp.concatenate([pool, conv], axis=1).astype(BF16)


def _pool_conv_sample(layer, u, c, pool_rows, conv_rows, pwin, wp, ps, wdw, bdw, lng, lnb, pw):
    bs = u.shape[0]
    tb = 32
    vec = _const_spec((1, CONV_WIDTH))
    return pl.pallas_call(
        _pool_conv_sample_kernel,
        grid=(bs // tb,),
        in_specs=[pl.BlockSpec((tb, POOL_WIDTH), lambda i: (i, 0)),
                  pl.BlockSpec((tb, CONV_WIDTH), lambda i: (i, 0)),
                  pl.BlockSpec((None, POOL_STATE, tb, POOL_WIDTH), lambda i: (layer, 0, i, 0)),
                  pl.BlockSpec((None, CONV_STATE, tb, CONV_WIDTH), lambda i: (layer, 0, i, 0)),
                  _const_spec(pwin.shape), _const_spec(wp.shape), vec, _const_spec(wdw.shape), vec, vec, vec,
                  _const_spec(pw.shape)],
        out_specs=pl.BlockSpec((tb, POOL_WIDTH + CONV_WIDTH), lambda i: (i, 0)),
        out_shape=jax.ShapeDtypeStruct((bs, POOL_WIDTH + CONV_WIDTH), BF16),
        name="pool_conv_sample",
    )(u, c, pool_rows, conv_rows, pwin, wp, ps, wdw, bdw, lng, lnb, pw)


def _out_proj_kernel(x_ref, a_ref, bc_ref, wo_ref, g_ref, wq_ref, x1_ref, q_ref):
    x1 = x_ref[...] + _dot(a_ref[...], wo_ref[0:SB_WIDTH, :]) + _dot(bc_ref[...], wo_ref[SB_WIDTH:, :])
    x1_ref[...] = x1
    h = _rms(x1, g_ref[...]).astype(BF16)
    q_ref[...] = (_dot(h, wq_ref[...]) * MEM_SCALE).astype(BF16)


def _out_proj(x, a, bc, wo, g, wq):
    rows = x.shape[0]
    tm = min(ROW_TILE, rows)

    def row_spec(width):
        return pl.BlockSpec((tm, width), lambda i: (i, 0))

    return pl.pallas_call(
        _out_proj_kernel,
        grid=(rows // tm,),
        in_specs=[row_spec(D_MODEL), row_spec(SB_WIDTH), row_spec(POOL_WIDTH + CONV_WIDTH),
                  _const_spec(wo.shape), _const_spec((1, D_MODEL)), _const_spec(wq.shape)],
        out_specs=[row_spec(D_MODEL), row_spec(D_MODEL)],
        out_shape=[jax.ShapeDtypeStruct((rows, D_MODEL), F32), jax.ShapeDtypeStruct((rows, D_MODEL), BF16)],
        name="out_proj",
    )(x, a, bc, wo, g, wq)


def _mem_attn_prompt_kernel(q_ref, mk_ref, mv_ref, o_ref):
    outs = []
    for h in range(MEM_HEADS):
        hs = slice(h * MEM_HEAD_DIM, (h + 1) * MEM_HEAD_DIM)
        s = _dot_nt(q_ref[0, :, hs], mk_ref[:, hs])
        e = jnp.exp(s - jnp.max(s, axis=-1, keepdims=True))
        den = jnp.sum(e, axis=-1, keepdims=True)
        outs.append(_dot((e / den).astype(BF16), mv_ref[:, hs]))
    o_ref[0] = jnp.concatenate(outs, axis=1).astype(BF16)


def _mem_attn_prompt(layer, q, mk, mv):
    b, seq, _ = q.shape
    tm = ROW_TILE
    mem_len = mk.shape[2]
    m_spec = pl.BlockSpec((None, None, mem_len, D_MODEL), lambda bi, i: (layer, bi, 0, 0))
    qo_spec = pl.BlockSpec((1, tm, D_MODEL), lambda bi, i: (bi, i, 0))
    return pl.pallas_call(
        _mem_attn_prompt_kernel,
        grid=(b, seq // tm),
        in_specs=[qo_spec, m_spec, m_spec],
        out_specs=qo_spec,
        out_shape=jax.ShapeDtypeStruct((b, seq, D_MODEL), BF16),
        name="mem_attn_prompt",
    )(q, mk, mv)


def _mem_attn_one(q, mk_ref, mv_ref):
    q = q.astype(F32)
    mem_len = mk_ref.shape[0]
    chunks = [(m0, m0 + MEM_CHUNK) for m0 in range(0, mem_len, MEM_CHUNK)]
    s = jnp.concatenate([jnp.sum(mk_ref[lo:hi] * q[None], axis=-1, keepdims=True) for lo, hi in chunks],
                        axis=0)
    e = jnp.exp(s - jnp.max(s, axis=0, keepdims=True))
    o = sum(jnp.sum(e[lo:hi] * mv_ref[lo:hi], axis=0) for lo, hi in chunks)
    return o / jnp.sum(e, axis=0)


def _mo_ffn_kernel(final, mem_per_step, x1_ref, o_ref, wmo_ref, g_ref, wup_ref, wdn_ref, *refs):
    refs = list(refs)
    gf_ref = refs.pop(0) if final else None
    if mem_per_step:
        qs_ref, mk_ref, mv_ref = refs[:3]
        out_ref, os_ref = refs[3:]
    else:
        (out_ref,) = refs
    x2 = x1_ref[...] + _dot(o_ref[...], wmo_ref[...])
    h = _rms(x2, g_ref[...]).astype(BF16)
    acc = x2
    n_chunks = D_FF // FF_CHUNK
    for k in range(n_chunks):
        for r in range(k * mem_per_step // n_chunks, (k + 1) * mem_per_step // n_chunks):
            os_ref[r] = _mem_attn_one(qs_ref[r], mk_ref.at[r], mv_ref.at[r]).astype(BF16)
        c0 = k * FF_CHUNK
        up = jnp.maximum(_dot(h, wup_ref[:, c0:c0 + FF_CHUNK]), 0.0)
        acc = acc + _dot((up * up).astype(BF16), wdn_ref[c0:c0 + FF_CHUNK, :])
    if final:
        acc = _rms(acc, gf_ref[...])
    out_ref[...] = acc


def _mo_ffn(x1, o, wmo, g, wup, wdn, g_final=None, mem=None):
    rows = x1.shape[0]
    tm = min(ROW_TILE, rows)
    steps = rows // tm
    row_spec = pl.BlockSpec((tm, D_MODEL), lambda i: (i, 0))
    final = g_final is not None
    in_specs = [row_spec, row_spec, _const_spec(wmo.shape), _const_spec((1, D_MODEL)), _const_spec(wup.shape),
                _const_spec(wdn.shape)]
    args = [x1, o, wmo, g, wup, wdn]
    out_specs = [row_spec]
    out_shape = [jax.ShapeDtypeStruct((rows, D_MODEL), F32)]
    if final:
        in_specs.append(_const_spec((1, D_MODEL)))
        args.append(g_final)
    per_step = 0
    if mem is not None:
        layer, qs, cache_k, cache_v = mem
        bs, mem_len = qs.shape[0], cache_k.shape[2]
        assert bs % steps == 0
        per_step = bs // steps
        vec_spec = pl.BlockSpec((per_step, MEM_HEADS, MEM_HEAD_DIM), lambda i: (i, 0, 0))
        m_spec = pl.BlockSpec((None, per_step, mem_len, MEM_HEADS, MEM_HEAD_DIM), lambda i: (layer, i, 0, 0, 0))
        in_specs += [vec_spec, m_spec, m_spec]
        args += [qs, cache_k, cache_v]
        out_specs.append(vec_spec)
        out_shape.append(jax.ShapeDtypeStruct((bs, MEM_HEADS, MEM_HEAD_DIM), BF16))
    outs = pl.pallas_call(
        functools.partial(_mo_ffn_kernel, final, per_step),
        grid=(steps,),
        in_specs=in_specs,
        out_specs=out_specs,
        out_shape=out_shape,
        name="mo_ffn",
    )(*args)
    return outs if mem is not None else outs[0]


def _suffix_triangle(n):
    j = lax.broadcasted_iota(jnp.int32, (n, n), 0)
    s = lax.broadcasted_iota(jnp.int32, (n, n), 1)
    return jnp.where(j > s, -1.0, 0.0).astype(BF16)


def _later_pages_matrix(n_pages):
    n = n_pages * SB_HEADS
    r = lax.broadcasted_iota(jnp.int32, (n, n), 0)
    c = lax.broadcasted_iota(jnp.int32, (n, n), 1)
    same_head = (r % SB_HEADS) == (c % SB_HEADS)
    return jnp.where(same_head & (c // SB_HEADS > r // SB_HEADS), -1.0, 0.0).astype(BF16)


def _pool_window_rows():
    row = lax.broadcasted_iota(jnp.int32, (POOL_STATE + 1, POOL_WIDTH), 0)
    ch = lax.broadcasted_iota(jnp.int32, (POOL_STATE + 1, POOL_WIDTH), 1)
    w = jnp.zeros((POOL_STATE + 1, POOL_WIDTH), jnp.int32)
    for g, win in enumerate(POOL_WINDOWS):
        w = jnp.where(ch // POOL_GROUP == g, win, w)
    return jnp.where(row >= POOL_STATE + 1 - w, 1.0 / w.astype(F32), 0.0)


def _block_diag(w_grp):
    n, g, _ = w_grp.shape
    out = jnp.zeros((n * g, n * g), w_grp.dtype)
    for i in range(n):
        out = lax.dynamic_update_slice(out, w_grp[i], (i * g, i * g))
    return out


def kernel(x_prompt, x_sample, cache_sb_k, cache_sb_v, cache_mem_k, cache_mem_v, state_pool, state_conv,
           page_table, mem_prompt, norm_mix_g, w_in, sb_bias, pool_w, pool_scale, conv_w, conv_b, conv_ln_g,
           conv_ln_b, conv_pw, w_out, norm_mem_g, w_mq, w_mk, w_mv, w_mo, norm_ffn_g, w_up, w_down,
           norm_final_g):
    bp, seq, _ = x_prompt.shape
    bs = x_sample.shape[0]
    n_pages = page_table.shape[1]
    n_phys = cache_sb_k.shape[1]
    mem_len = mem_prompt.shape[1]
    assert x_sample.shape[1] == 1 and seq % SB_TILE == 0 and seq >= CONV_STATE

    cache_k = cache_sb_k.transpose(0, 1, 3, 4, 2).reshape(DEPTH, n_phys, SB_WIDTH, PAGE_SIZE)
    cache_v = cache_sb_v.transpose(0, 1, 3, 4, 2).reshape(DEPTH, n_phys, SB_WIDTH, PAGE_SIZE)
    pool_rows = state_pool.transpose(0, 2, 1, 3)
    conv_rows = state_conv.transpose(0, 2, 1, 3)

    tri_prompt = _suffix_triangle(SB_TILE)
    tri_page = _suffix_triangle(PAGE_SIZE)
    pages_mat = _later_pages_matrix(n_pages)
    pwin = _pool_window_rows()

    mk_all, mv_all, mkb_all, mvb_all = _mem_kv(mem_prompt.reshape(bp * mem_len, D_MODEL),
                                                w_mk.astype(BF16), w_mv.astype(BF16))
    mkb_all = mkb_all.reshape(DEPTH, bp, mem_len, D_MODEL)
    mvb_all = mvb_all.reshape(DEPTH, bp, mem_len, D_MODEL)

    def heads_last(t):
        return t.reshape(t.shape[0], SB_HEADS, SB_HEAD_DIM, t.shape[2]).transpose(0, 3, 1, 2)

    xp = x_prompt
    xs = x_sample
    sbk_p, sbv_p, sbk_s, sbv_s = [], [], [], []
    pool_p, pool_s, conv_p, conv_s = [], [], [], []
    for l in range(DEPTH):
        last = l == DEPTH - 1
        g_mix = norm_mix_g[l].reshape(1, D_MODEL)
        g_mem = norm_mem_g[l].reshape(1, D_MODEL)
        g_ffn = norm_ffn_g[l].reshape(1, D_MODEL)
        g_fin = norm_final_g.reshape(1, D_MODEL) if last else None
        wq_sb = w_in[l, :, 0:SB_WIDTH].astype(BF16)
        wkt = w_in[l, :, SB_WIDTH:2 * SB_WIDTH].T.astype(BF16)
        wvt = w_in[l, :, 2 * SB_WIDTH:3 * SB_WIDTH].T.astype(BF16)
        wr = w_in[l, :, 3 * SB_WIDTH:].astype(BF16)
        wo, wq, wmo = (w.astype(BF16) for w in (w_out[l], w_mq[l], w_mo[l]))
        wup, wdn = w_up[l].astype(BF16), w_down[l].astype(BF16)
        wp = _block_diag(pool_w[l]).astype(BF16)
        ps = pool_scale[l].reshape(1, POOL_WIDTH)
        bdw = conv_b[l].reshape(1, CONV_WIDTH)
        lng = conv_ln_g[l].reshape(1, CONV_WIDTH)
        lnb = conv_ln_b[l].reshape(1, CONV_WIDTH)
        pw = conv_pw[l].astype(BF16)
        bias2 = sb_bias[l] * LOG2E
        bias_rows = jnp.broadcast_to(jnp.tile(bias2, n_pages)[:, None], (n_pages * SB_HEADS, LANES))

        q, kt, vt, ktb, vtb, u, c = _in_proj(xp.reshape(bp, seq, D_MODEL), g_mix, wq_sb, wkt, wvt, wr)
        sbk_p.append(heads_last(kt))
        sbv_p.append(heads_last(vt))
        pool_p.append(u[:, seq - POOL_STATE:])
        conv_p.append(c[:, seq - CONV_STATE:])
        q_s, kt_s, vt_s, _, _, u_s, c_s = _in_proj(xs.reshape(1, bs, D_MODEL), g_mix, wq_sb, wkt, wvt, wr)
        sbk_s.append(heads_last(kt_s).reshape(bs, 1, SB_HEADS, SB_HEAD_DIM))
        sbv_s.append(heads_last(vt_s).reshape(bs, 1, SB_HEADS, SB_HEAD_DIM))
        u_s, c_s = u_s[0], c_s[0]
        pool_s.append(jnp.concatenate([state_pool[l, :, 1:], u_s[:, None]], axis=1))
        conv_s.append(jnp.concatenate([state_conv[l, :, 1:], c_s[:, None]], axis=1))
        a, a_s = _sb_attention(l, bias2, q, ktb, vtb, tri_prompt, page_table, q_s.reshape(bs, 1, SB_WIDTH),
                               bias_rows, tri_page, pages_mat, cache_k, cache_v)

        bc_s = _pool_conv_sample(l, u_s, c_s, pool_rows, conv_rows, pwin, wp, ps, conv_w[l], bdw, lng, lnb, pw)
        x1_s, qm_s = _out_proj(xs.reshape(bs, D_MODEL), a_s.reshape(bs, SB_WIDTH).astype(BF16), bc_s, wo, g_mem, wq)

        bc = _pool_conv_prompt(u, c, wp, ps, conv_w[l], bdw, lng, lnb, pw)
        x1, qm = _out_proj(xp.reshape(bp * seq, D_MODEL), a.reshape(bp * seq, SB_WIDTH),
                           bc.reshape(bp * seq, POOL_WIDTH + CONV_WIDTH), wo, g_mem, wq)
        o = _mem_attn_prompt(l, qm.reshape(bp, seq, D_MODEL), mkb_all, mvb_all)
        xp, o_s = _mo_ffn(x1, o.reshape(bp * seq, D_MODEL), wmo, g_ffn, wup, wdn, g_fin,
                          mem=(l, qm_s.reshape(bs, MEM_HEADS, MEM_HEAD_DIM), cache_mem_k, cache_mem_v))
        xs = _mo_ffn(x1_s, o_s.reshape(bs, D_MODEL), wmo, g_ffn, wup, wdn, g_fin)

    return (xp.reshape(bp, seq, D_MODEL), xs.reshape(bs, 1, D_MODEL),
            jnp.stack(sbk_p), jnp.stack(sbv_p), jnp.stack(sbk_s), jnp.stack(sbv_s),
            mk_all.reshape(DEPTH, bp, mem_len, MEM_HEADS, MEM_HEAD_DIM),
            mv_all.reshape(DEPTH, bp, mem_len, MEM_HEADS, MEM_HEAD_DIM),
            jnp.stack(pool_p), jnp.stack(pool_s), jnp.stack(conv_p), jnp.stack(conv_s))
```
